```python
import math
import jax, jax.numpy as jnp
from jax import lax
import numpy as np

D_MODEL = 4096
BATCH = 16
SEQ = 256
DEPTH = 4
DEC_BATCH = 8
DEC_SEQ = 1024
PAST_LEN = 512

GRID_W = 64
D_SSM = D_MODEL
SSM_HEAD_DIM = 64
SSM_HEADS = D_SSM // SSM_HEAD_DIM
SSM_GROUPS = 8
HEADS_PER_GROUP = SSM_HEADS // SSM_GROUPS
SSM_STATE = 128
SSM_CONV_W = 3
CHUNK = 128
D_CONV = D_MODEL
CONV_W = 31
D_FF = 11008
FFN_CONV_W = 3
D_XBC = D_SSM + 2 * SSM_GROUPS * SSM_STATE
Z_END = D_SSM
XBC_END = Z_END + D_XBC
DT_END = XBC_END + 2 * SSM_HEADS
GLU_END = DT_END + 2 * D_CONV
IN_COLS = GLU_END + 2 * D_MODEL
EPS = 1e-6

kernel_name = "hybrid_ssd_conformer_diffusion_step"


def rmsnorm(x, g):
    xf = x.astype(jnp.float32)
    y = xf * lax.rsqrt(jnp.mean(xf * xf, axis=-1, keepdims=True) + EPS)
    return (y * g).astype(x.dtype)


def layernorm(x, g, b):
    xf = x.astype(jnp.float32)
    xc = xf - jnp.mean(xf, axis=-1, keepdims=True)
    var = jnp.mean(xc * xc, axis=-1, keepdims=True)
    return (xc * lax.rsqrt(var + EPS) * g + b).astype(x.dtype)


def dwconv(x, w, b, rows):
    bsz, length, ch = x.shape
    if rows is not None:
        x = x.reshape(bsz * rows, GRID_W, ch)
    k = w.shape[0]
    y = lax.conv_general_dilated(
        x, w[:, None, :].astype(x.dtype), window_strides=(1,),
        padding=[(k // 2, k // 2)], dimension_numbers=("NWC", "WIO", "NWC"),
        feature_group_count=ch)
    return (y + b).reshape(bsz, length, ch)


def ssd_scan(x, dt, a_neg, bm, cm, h0):
    bsz, length, g, r, p = x.shape
    n = bm.shape[-1]
    nc = length // CHUNK
    xc = x.reshape(bsz, nc, CHUNK, g, r, p)
    dtc = dt.reshape(bsz, nc, CHUNK, g, r)
    bc = bm.reshape(bsz, nc, CHUNK, g, n)
    cc = cm.reshape(bsz, nc, CHUNK, g, n)
    cum = jnp.cumsum(dtc.astype(jnp.float32) * a_neg.astype(jnp.float32), axis=2)
    causal = jnp.tril(jnp.ones((CHUNK, CHUNK), dtype=bool))[None, None, :, :, None, None]
    seg = jnp.where(causal, cum[:, :, :, None] - cum[:, :, None, :], -jnp.inf)
    decay = jnp.exp(seg).astype(x.dtype)
    scores = jnp.einsum("bcign,bcjgn->bcijg", cc, bc)
    w_ij = scores[..., None] * decay * dtc[:, :, None]
    y_diag = jnp.einsum("bcijgr,bcjgrp->bcigrp", w_ij, xc)
    to_end = jnp.exp(cum[:, :, -1:] - cum).astype(x.dtype) * dtc
    chunk_states = jnp.einsum("bcjgn,bcjgr,bcjgrp->bcgrpn", bc, to_end, xc)
    chunk_decay = jnp.exp(cum[:, :, -1]).astype(chunk_states.dtype)

    def step(h, inp):
        st, dec = inp
        return dec[..., None, None] * h + st, h

    h_last, h_in = lax.scan(
        step, h0.astype(chunk_states.dtype),
        (jnp.moveaxis(chunk_states, 1, 0), jnp.moveaxis(chunk_decay, 1, 0)))
    h_in = jnp.moveaxis(h_in, 0, 1)
    y_off = jnp.einsum("bcign,bcgrpn,bcigr->bcigrp", cc, h_in, jnp.exp(cum).astype(x.dtype))
    return (y_diag + y_off).reshape(bsz, length, g, r, p), h_last


def mixer(h, lp, h0_f, h0_b, rows):
    bsz, length, _ = h.shape
    proj = h @ lp["w_in"]
    z, xbc, dt_raw, glu, gates = jnp.split(proj, [Z_END, XBC_END, DT_END, GLU_END], axis=-1)
    xbc = jax.nn.silu(dwconv(xbc, lp["ssm_conv_w"], lp["ssm_conv_b"], rows))
    xs, bm, cm = jnp.split(xbc, [D_SSM, D_SSM + SSM_GROUPS * SSM_STATE], axis=-1)
    xs = xs.reshape(bsz, length, SSM_GROUPS, HEADS_PER_GROUP, SSM_HEAD_DIM)
    bm = bm.reshape(bsz, length, SSM_GROUPS, SSM_STATE)
    cm = cm.reshape(bsz, length, SSM_GROUPS, SSM_STATE)
    dt = jax.nn.softplus(dt_raw.astype(jnp.float32).reshape(bsz, length, 2, SSM_HEADS)
                         + lp["dt_bias"].astype(jnp.float32)).astype(h.dtype)
    dt = dt.reshape(bsz, length, 2, SSM_GROUPS, HEADS_PER_GROUP)
    a_neg = -jnp.exp(lp["a_log"].astype(jnp.float32)).reshape(2, SSM_GROUPS, HEADS_PER_GROUP)
    y_f, hf = ssd_scan(xs, dt[:, :, 0], a_neg[0], bm, cm, h0_f)
    y_b, hb = ssd_scan(jnp.flip(xs, 1), jnp.flip(dt[:, :, 1], 1), a_neg[1],
                       jnp.flip(bm, 1), jnp.flip(cm, 1), h0_b)
    y = y_f + jnp.flip(y_b, 1) + lp["d_skip"].reshape(SSM_GROUPS, HEADS_PER_GROUP, 1) * xs
    y = rmsnorm(y.reshape(bsz, length, D_SSM) * jax.nn.silu(z), lp["g_ssm_norm"])
    branch_ssm = y @ lp["w_br_ssm"]
    u_a, u_g = jnp.split(glu, 2, axis=-1)
    u = dwconv(u_a * jax.nn.sigmoid(u_g), lp["cv_conv_w"], lp["cv_conv_b"], rows)
    u = jax.nn.silu(layernorm(u, lp["cv_ln_g"], lp["cv_ln_b"]))
    branch_cv = u @ lp["w_cv_out"]
    gate_ssm, gate_cv = jnp.split(gates, 2, axis=-1)
    merged = jax.nn.sigmoid(gate_ssm) * branch_ssm + jax.nn.sigmoid(gate_cv) * branch_cv
    return merged @ lp["w_out"], hf, hb


def block(x, mod, lp, h0_f, h0_b, rows):
    sh1, sc1, gt1, sh2, sc2, gt2 = jnp.split(mod, 6, axis=-1)
    h = rmsnorm(x, lp["g_pre_mix"]) * (1 + sc1) + sh1
    m, hf, hb = mixer(h, lp, h0_f, h0_b, rows)
    x = x + gt1 * rmsnorm(m, lp["g_post_mix"])
    h = rmsnorm(x, lp["g_pre_ffn"]) * (1 + sc2) + sh2
    u = dwconv(h @ lp["w_up"], lp["ffn_conv_w"], lp["ffn_conv_b"], rows)
    u_gate, u_val = jnp.split(u, 2, axis=-1)
    f = (jax.nn.silu(u_gate) * u_val) @ lp["w_down"]
    x = x + gt2 * rmsnorm(f, lp["g_post_ffn"])
    return x, hf, hb


def setup_inputs(seed: int = 0) -> dict:
    key = jax.random.key(seed)
    ks = jax.random.split(key, 40)
    f32 = jnp.float32

    def nrm(k, shape, scale):
        return jax.random.normal(k, shape, f32) * scale

    dt0 = jnp.exp(jax.random.uniform(ks[12], (DEPTH, 2, SSM_HEADS), f32,
                                     math.log(1e-3), math.log(1e-1)))
    return {
        "x_prompt": nrm(ks[0], (BATCH, SEQ, D_MODEL), 1.0),
        "x_sample": nrm(ks[1], (DEC_BATCH, DEC_SEQ, D_MODEL), 1.0),
        "state_ssm": nrm(ks[2], (DEC_BATCH, DEPTH, 2, SSM_HEADS, SSM_HEAD_DIM, SSM_STATE), 0.1),
        "c": nrm(ks[3], (DEC_BATCH, D_MODEL), 1.0),
        "c_ctx": nrm(ks[4], (D_MODEL,), 1.0),
        "w_mod": nrm(ks[5], (DEPTH, D_MODEL, 6 * D_MODEL), 0.5 * D_MODEL ** -0.5),
        "b_mod": nrm(ks[6], (DEPTH, 6 * D_MODEL), 0.02),
        "g_pre_mix": 1.0 + nrm(ks[7], (DEPTH, D_MODEL), 0.02),
        "g_post_mix": 1.0 + nrm(ks[8], (DEPTH, D_MODEL), 0.02),
        "g_pre_ffn": 1.0 + nrm(ks[9], (DEPTH, D_MODEL), 0.02),
        "g_post_ffn": 1.0 + nrm(ks[10], (DEPTH, D_MODEL), 0.02),
        "w_in": nrm(ks[11], (DEPTH, D_MODEL, IN_COLS), D_MODEL ** -0.5),
        "ssm_conv_w": nrm(ks[13], (DEPTH, SSM_CONV_W, D_XBC), SSM_CONV_W ** -0.5),
        "ssm_conv_b": nrm(ks[14], (DEPTH, D_XBC), 0.02),
        "dt_bias": dt0 + jnp.log(-jnp.expm1(-dt0)),
        "a_log": jnp.log(jax.random.uniform(ks[15], (DEPTH, 2, SSM_HEADS), f32, 1.0, 16.0)),
        "d_skip": 1.0 + nrm(ks[16], (DEPTH, SSM_HEADS), 0.1),
        "g_ssm_norm": 1.0 + nrm(ks[17], (DEPTH, D_SSM), 0.02),
        "w_br_ssm": nrm(ks[18], (DEPTH, D_SSM, D_MODEL), D_SSM ** -0.5),
        "cv_conv_w": nrm(ks[19], (DEPTH, CONV_W, D_CONV), CONV_W ** -0.5),
        "cv_conv_b": nrm(ks[20], (DEPTH, D_CONV), 0.02),
        "cv_ln_g": 1.0 + nrm(ks[21], (DEPTH, D_CONV), 0.02),
        "cv_ln_b": nrm(ks[22], (DEPTH, D_CONV), 0.02),
        "w_cv_out": nrm(ks[23], (DEPTH, D_CONV, D_MODEL), D_CONV ** -0.5),
        "w_out": nrm(ks[24], (DEPTH, D_MODEL, D_MODEL), D_MODEL ** -0.5),
        "w_up": nrm(ks[25], (DEPTH, D_MODEL, 2 * D_FF), D_MODEL ** -0.5),
        "ffn_conv_w": nrm(ks[26], (DEPTH, FFN_CONV_W, 2 * D_FF), FFN_CONV_W ** -0.5),
        "ffn_conv_b": nrm(ks[27], (DEPTH, 2 * D_FF), 0.02),
        "w_down": nrm(ks[28], (DEPTH, D_FF, D_MODEL), D_FF ** -0.5),
    }


def reference(x_prompt, x_sample, state_ssm, c, c_ctx, w_mod, b_mod, g_pre_mix, g_post_mix,
              g_pre_ffn, g_post_ffn, w_in, ssm_conv_w, ssm_conv_b, dt_bias, a_log, d_skip,
              g_ssm_norm, w_br_ssm, cv_conv_w, cv_conv_b, cv_ln_g, cv_ln_b, w_cv_out, w_out,
              w_up, ffn_conv_w, ffn_conv_b, w_down):
    n_ctx_req = x_prompt.shape[0]
    n_lat_req, lat_len, _ = x_sample.shape
    rows = lat_len // GRID_W
    head_shape = (SSM_GROUPS, HEADS_PER_GROUP, SSM_HEAD_DIM, SSM_STATE)
    zero_state = jnp.zeros((n_ctx_req,) + head_shape, x_prompt.dtype)
    y_prompt, y_sample = x_prompt, x_sample
    ctx_states = []
    for l in range(DEPTH):
        lp = {
            "g_pre_mix": g_pre_mix[l], "g_post_mix": g_post_mix[l],
            "g_pre_ffn": g_pre_ffn[l], "g_post_ffn": g_post_ffn[l],
            "w_in": w_in[l], "ssm_conv_w": ssm_conv_w[l], "ssm_conv_b": ssm_conv_b[l],
            "dt_bias": dt_bias[l], "a_log": a_log[l], "d_skip": d_skip[l],
            "g_ssm_norm": g_ssm_norm[l], "w_br_ssm": w_br_ssm[l],
            "cv_conv_w": cv_conv_w[l], "cv_conv_b": cv_conv_b[l],
            "cv_ln_g": cv_ln_g[l], "cv_ln_b": cv_ln_b[l], "w_cv_out": w_cv_out[l],
            "w_out": w_out[l], "w_up": w_up[l], "ffn_conv_w": ffn_conv_w[l],
            "ffn_conv_b": ffn_conv_b[l], "w_down": w_down[l],
        }
        mod_ctx = jax.nn.silu(c_ctx) @ w_mod[l] + b_mod[l]
        mod_lat = (jax.nn.silu(c) @ w_mod[l] + b_mod[l])[:, None, :]
        y_prompt, hf, hb = block(y_prompt, mod_ctx, lp, zero_state, zero_state, None)
        ctx_states.append(jnp.stack([hf, hb], axis=1).reshape(
            n_ctx_req, 2, SSM_HEADS, SSM_HEAD_DIM, SSM_STATE))
        cached = state_ssm[:, l].reshape((n_lat_req, 2) + head_shape)
        y_sample, _, _ = block(y_sample, mod_lat, lp, cached[:, 0], cached[:, 1], rows)
    new_state_ssm = jnp.stack(ctx_states, axis=1)
    return (y_prompt, y_sample, new_state_ssm)
```

```python
import functools

import jax
import jax.numpy as jnp
from jax import lax
from jax.experimental import pallas as pl
from jax.experimental.pallas import tpu as pltpu

GRID_W = 64
CHUNK = 128
EPS = 1e-6
ROW_BLOCK = 256
CONV_PAD = 16
CONV_PIECE = 64
MOD_ROWS = 16
V7X_VMEM_LIMIT = 56 * 1024 * 1024

F32 = jnp.float32
BF16 = jnp.bfloat16


def _params(*sem):
    return pltpu.CompilerParams(dimension_semantics=sem, vmem_limit_bytes=V7X_VMEM_LIMIT)


def _tile(n, pref):
    if n <= pref:
        return n
    t = (pref // 128) * 128
    while n % t:
        t -= 128
    return t


def _sigmoid(x):
    return 1.0 / (1.0 + jnp.exp(-x))


def _silu(x):
    return x * _sigmoid(x)


def _softplus(x):
    return jnp.maximum(x, 0.0) + jnp.log1p(jnp.exp(-jnp.abs(x)))


def _rms(x, g):
    return x * lax.rsqrt(jnp.mean(x * x, axis=-1, keepdims=True) + EPS) * g


def _mm_kernel(x_ref, w_ref, o_ref):
    o_ref[...] = jnp.dot(x_ref[...], w_ref[...], preferred_element_type=F32).astype(o_ref.dtype)


def _matmul(x, w, *, tm, tn, out_dtype=F32):
    m, k = x.shape
    n = w.shape[1]
    tm, tn = _tile(m, tm), _tile(n, tn)
    return pl.pallas_call(
        _mm_kernel,
        grid=(m // tm, n // tn),
        in_specs=[pl.BlockSpec((tm, k), lambda i, j: (i, 0)),
                  pl.BlockSpec((k, tn), lambda i, j: (0, j))],
        out_specs=pl.BlockSpec((tm, tn), lambda i, j: (i, j)),
        out_shape=jax.ShapeDtypeStruct((m, n), out_dtype),
        compiler_params=_params("parallel", "parallel"),
    )(x, w)


def _merge_kernel(y_ref, u_ref, w1_ref, w2_ref, g1_ref, g2_ref, o_ref):
    a = jnp.dot(y_ref[...], w1_ref[...], preferred_element_type=F32)
    b = jnp.dot(u_ref[...], w2_ref[...], preferred_element_type=F32)
    o_ref[...] = (_sigmoid(g1_ref[...]) * a + _sigmoid(g2_ref[...]) * b).astype(o_ref.dtype)


def _merge(yn, ucv, w1, w2, gg, gate_col0, *, tm, tn):
    m, k = yn.shape
    n = w1.shape[1]
    tm, tn = _tile(m, tm), _tile(n, tn)
    c1 = gate_col0 // tn
    c2 = (gate_col0 + n) // tn
    return pl.pallas_call(
        _merge_kernel,
        grid=(m // tm, n // tn),
        in_specs=[pl.BlockSpec((tm, k), lambda i, j: (i, 0)),
                  pl.BlockSpec((tm, k), lambda i, j: (i, 0)),
                  pl.BlockSpec((k, tn), lambda i, j: (0, j)),
                  pl.BlockSpec((k, tn), lambda i, j: (0, j)),
                  pl.BlockSpec((tm, tn), lambda i, j: (i, c1 + j)),
                  pl.BlockSpec((tm, tn), lambda i, j: (i, c2 + j))],
        out_specs=pl.BlockSpec((tm, tn), lambda i, j: (i, j)),
        out_shape=jax.ShapeDtypeStruct((m, n), BF16),
        compiler_params=_params("parallel", "parallel"),
    )(yn, ucv, w1, w2, gg, gg)


def _mod_kernel(c_ref, w_ref, b_ref, o_ref):
    cs = _silu(c_ref[...]).astype(BF16)
    o_ref[...] = jnp.dot(cs, w_ref[...].astype(BF16), preferred_element_type=F32) + b_ref[...]


def _modulation(cvec, w_mod, b_mod):
    depth, d, n = w_mod.shape
    tn = _tile(n, 512)
    return pl.pallas_call(
        _mod_kernel,
        grid=(depth, n // tn),
        in_specs=[pl.BlockSpec((MOD_ROWS, d), lambda l, j: (0, 0)),
                  pl.BlockSpec((None, d, tn), lambda l, j: (l, 0, j)),
                  pl.BlockSpec((None, 1, tn), lambda l, j: (l, 0, j))],
        out_specs=pl.BlockSpec((None, MOD_ROWS, tn), lambda l, j: (l, 0, j)),
        out_shape=jax.ShapeDtypeStruct((depth, MOD_ROWS, n), F32),
        compiler_params=_params("parallel", "parallel"),
    )(cvec, w_mod, b_mod.reshape(depth, 1, n))


def _prenorm_kernel(x_ref, g_ref, sc_ref, sh_ref, h_ref):
    h_ref[...] = (_rms(x_ref[...], g_ref[...]) * (1.0 + sc_ref[...]) + sh_ref[...]).astype(h_ref.dtype)


def _resnorm_kernel(x_ref, r_ref, ga_ref, gt_ref, gb_ref, sc_ref, sh_ref, xo_ref, h_ref):
    xn = x_ref[...] + gt_ref[...] * _rms(r_ref[...], ga_ref[...])
    xo_ref[...] = xn
    h_ref[...] = (_rms(xn, gb_ref[...]) * (1.0 + sc_ref[...]) + sh_ref[...]).astype(h_ref.dtype)


def _res_kernel(x_ref, r_ref, ga_ref, gt_ref, xo_ref):
    xo_ref[...] = x_ref[...] + gt_ref[...] * _rms(r_ref[...], ga_ref[...])


def _mod_spec(d, which, n_ctx_blk, lat_blk_per_req):
    def index(i):
        row = (i >= n_ctx_blk).astype(jnp.int32) * (1 + (i - n_ctx_blk) // lat_blk_per_req)
        return (row * 6 + which, 0, 0)
    return pl.BlockSpec((None, 1, d), index)


def _row_spec(d):
    return pl.BlockSpec((ROW_BLOCK, d), lambda i: (i, 0))


def _vec_spec(d):
    return pl.BlockSpec((1, d), lambda i: (0, 0))


def _prenorm(x, g, mods, sc_idx, sh_idx, geo):
    t, d = x.shape
    ms = functools.partial(_mod_spec, d, n_ctx_blk=geo[0], lat_blk_per_req=geo[1])
    return pl.pallas_call(
        _prenorm_kernel,
        grid=(t // ROW_BLOCK,),
        in_specs=[_row_spec(d), _vec_spec(d), ms(sc_idx), ms(sh_idx)],
        out_specs=_row_spec(d),
        out_shape=jax.ShapeDtypeStruct((t, d), BF16),
        compiler_params=_params("parallel"),
    )(x, g.reshape(1, d), mods, mods)


def _resnorm(x, r, ga, mods_gate, gt_idx, gb, mods_next, sc_idx, sh_idx, geo):
    t, d = x.shape
    ms = functools.partial(_mod_spec, d, n_ctx_blk=geo[0], lat_blk_per_req=geo[1])
    return pl.pallas_call(
        _resnorm_kernel,
        grid=(t // ROW_BLOCK,),
        in_specs=[_row_spec(d), _row_spec(d), _vec_spec(d), ms(gt_idx), _vec_spec(d), ms(sc_idx), ms(sh_idx)],
        out_specs=[_row_spec(d), _row_spec(d)],
        out_shape=[jax.ShapeDtypeStruct((t, d), F32), jax.ShapeDtypeStruct((t, d), BF16)],
        compiler_params=_params("parallel"),
    )(x, r, ga.reshape(1, d), mods_gate, gb.reshape(1, d), mods_next, mods_next)


def _res(x, r, ga, mods_gate, gt_idx, geo):
    t, d = x.shape
    ms = functools.partial(_mod_spec, d, n_ctx_blk=geo[0], lat_blk_per_req=geo[1])
    return pl.pallas_call(
        _res_kernel,
        grid=(t // ROW_BLOCK,),
        in_specs=[_row_spec(d), _row_spec(d), _vec_spec(d), ms(gt_idx)],
        out_specs=_row_spec(d),
        out_shape=jax.ShapeDtypeStruct((t, d), F32),
        compiler_params=_params("parallel"),
    )(x, r, ga.reshape(1, d), mods_gate)


def _gated_norm_kernel(y_ref, z_ref, g_ref, o_ref):
    o_ref[...] = _rms(y_ref[...] * _silu(z_ref[...]), g_ref[...]).astype(o_ref.dtype)


def _gated_norm(y, z, g):
    t, d = y.shape
    return pl.pallas_call(
        _gated_norm_kernel,
        grid=(t // ROW_BLOCK,),
        in_specs=[_row_spec(d), _row_spec(d), _vec_spec(d)],
        out_specs=_row_spec(d),
        out_shape=jax.ShapeDtypeStruct((t, d), BF16),
        compiler_params=_params("parallel"),
    )(y, z, g.reshape(1, d))


def _ln_silu_kernel(u_ref, g_ref, b_ref, o_ref):
    u = u_ref[...]
    uc = u - jnp.mean(u, axis=-1, keepdims=True)
    var = jnp.mean(uc * uc, axis=-1, keepdims=True)
    o_ref[...] = _silu(uc * lax.rsqrt(var + EPS) * g_ref[...] + b_ref[...]).astype(o_ref.dtype)


def _ln_silu(u, g, b):
    t, d = u.shape
    return pl.pallas_call(
        _ln_silu_kernel,
        grid=(t // ROW_BLOCK,),
        in_specs=[_row_spec(d), _vec_spec(d), _vec_spec(d)],
        out_specs=_row_spec(d),
        out_shape=jax.ShapeDtypeStruct((t, d), BF16),
        compiler_params=_params("parallel"),
    )(u, g.reshape(1, d), b.reshape(1, d))


def _seg_len(n_ctx_blk, seq):
    return jnp.where(pl.program_id(0) < n_ctx_blk, seq, GRID_W)


def _conv3_masks(r0, seg):
    pos = (lax.broadcasted_iota(jnp.int32, (CONV_PIECE, 128), 0) + r0) & (seg - 1)
    return pos != 0, pos != seg - 1


def _conv3_unit(x_ref, w_ref, b_ref, r0, c0, keep_prev, keep_next):
    n = CONV_PIECE
    cs = slice(c0, c0 + 128)
    x = x_ref[r0:r0 + n, cs]
    prev = x_ref[r0 - 1:r0 + n - 1, cs] if r0 > 0 else pltpu.roll(x, 1, 0)
    nxt = x_ref[r0 + 1:r0 + n + 1, cs] if r0 + n < x_ref.shape[0] else pltpu.roll(x, n - 1, 0)
    acc = w_ref[1:2, cs] * x + b_ref[:, cs]
    acc = acc + w_ref[0:1, cs] * jnp.where(keep_prev, prev, 0.0)
    return acc + w_ref[2:3, cs] * jnp.where(keep_next, nxt, 0.0)


def _ffn_act_kernel(ug_ref, uv_ref, wg_ref, wv_ref, bg_ref, bv_ref, o_ref, *, n_ctx_blk, seq):
    seg = _seg_len(n_ctx_blk, seq)
    rows, cols = o_ref.shape
    for r0 in range(0, rows, CONV_PIECE):
        kp, kn = _conv3_masks(r0, seg)
        for c0 in range(0, cols, 128):
            gate = _conv3_unit(ug_ref, wg_ref, bg_ref, r0, c0, kp, kn)
            val = _conv3_unit(uv_ref, wv_ref, bv_ref, r0, c0, kp, kn)
            o_ref[r0:r0 + CONV_PIECE, c0:c0 + 128] = (_silu(gate) * val).astype(o_ref.dtype)


def _ffn_act(u, w, b, n_ctx_rows, seq, *, tr, tc):
    t, f2 = u.shape
    f = f2 // 2
    tr, tc = _tile(n_ctx_rows, tr), _tile(f, tc)
    nj = f // tc
    kw = w.shape[0]
    b2 = b.reshape(1, f2)
    return pl.pallas_call(
        functools.partial(_ffn_act_kernel, n_ctx_blk=n_ctx_rows // tr, seq=seq),
        grid=(t // tr, nj),
        in_specs=[pl.BlockSpec((tr, tc), lambda i, j: (i, j)),
                  pl.BlockSpec((tr, tc), lambda i, j: (i, nj + j)),
                  pl.BlockSpec((kw, tc), lambda i, j: (0, j)),
                  pl.BlockSpec((kw, tc), lambda i, j: (0, nj + j)),
                  pl.BlockSpec((1, tc), lambda i, j: (0, j)),
                  pl.BlockSpec((1, tc), lambda i, j: (0, nj + j))],
        out_specs=pl.BlockSpec((tr, tc), lambda i, j: (i, j)),
        out_shape=jax.ShapeDtypeStruct((t, f), BF16),
        compiler_params=_params("parallel", "parallel"),
    )(u, u, w, w, b2, b2)


def _glu_conv_kernel(ua_ref, ug_ref, w_ref, b_ref, o_ref, pad_ref, *, n_ctx_blk, seq, kw):
    rows, cols = o_ref.shape
    half = kw // 2
    piece = min(GRID_W, rows)

    def run(seg):
        stride = seg + 2 * CONV_PAD
        used = (rows // seg) * stride
        zeros = jnp.zeros((CONV_PAD, cols), F32)
        for s in range(rows // seg):
            base = s * stride
            src = slice(s * seg, (s + 1) * seg)
            pad_ref[0, base:base + CONV_PAD, :] = zeros
            pad_ref[0, base + CONV_PAD:base + CONV_PAD + seg, :] = ua_ref[src, :] * _sigmoid(ug_ref[src, :])
            pad_ref[0, base + CONV_PAD + seg:base + stride, :] = zeros
        for r in range(1, 8):
            pad_ref[r, 0:used - 8, :] = pad_ref[0, r:used - 8 + r, :]
        for s in range(rows // seg):
            for p in range(seg // piece):
                a = s * stride + p * piece
                acc = jnp.zeros((piece, cols), F32)
                for k in range(kw):
                    off = CONV_PAD - half + k
                    acc = acc + w_ref[k:k + 1, :] * pad_ref[off % 8, a + off - off % 8:a + off - off % 8 + piece, :]
                out0 = s * seg + p * piece
                o_ref[out0:out0 + piece, :] = acc + b_ref[...]

    is_ctx = pl.program_id(0) < n_ctx_blk

    @pl.when(is_ctx)
    def _():
        run(min(seq, rows))

    @pl.when(jnp.logical_not(is_ctx))
    def _():
        run(GRID_W)


def _glu_conv(gg, w, b, n_ctx_rows, seq, *, tc):
    t = gg.shape[0]
    kw, c = w.shape
    assert kw // 2 <= CONV_PAD
    tr = ROW_BLOCK
    tc = _tile(c, tc)
    nj = c // tc
    pad_rows = (tr // GRID_W) * (GRID_W + 2 * CONV_PAD)
    return pl.pallas_call(
        functools.partial(_glu_conv_kernel, n_ctx_blk=n_ctx_rows // tr, seq=seq, kw=kw),
        grid=(t // tr, nj),
        in_specs=[pl.BlockSpec((tr, tc), lambda i, j: (i, j)),
                  pl.BlockSpec((tr, tc), lambda i, j: (i, nj + j)),
                  pl.BlockSpec((kw, tc), lambda i, j: (0, j)),
                  pl.BlockSpec((1, tc), lambda i, j: (0, j))],
        out_specs=pl.BlockSpec((tr, tc), lambda i, j: (i, j)),
        out_shape=jax.ShapeDtypeStruct((t, c), F32),
        scratch_shapes=[pltpu.VMEM((8, pad_rows, tc), F32)],
        compiler_params=_params("parallel", "parallel"),
    )(gg, gg, w, b.reshape(1, c))


def _split3(v, width):
    lane = lax.broadcasted_iota(jnp.int32, v.shape, 1)
    r1 = v - v.astype(BF16).astype(F32)
    r2 = r1 - r1.astype(BF16).astype(F32)
    return jnp.where(lane < width, v, jnp.where(lane < 2 * width, r1, r2)).astype(BF16)


def _head_selector(d, heads, reps, out_lanes):
    assert reps & (reps - 1) == 0 and heads & (heads - 1) == 0
    k = lax.broadcasted_iota(jnp.int32, (6 * heads, out_lanes), 0) & (2 * heads - 1)
    m = lax.shift_right_logical(lax.broadcasted_iota(jnp.int32, (6 * heads, out_lanes), 1), reps.bit_length() - 1)
    return (k == d * heads + m).astype(BF16)


def _ssd_kernel(x_ref, b_ref, c_ref, wx_ref, wb_ref, wc_ref, cbx_ref, cbb_ref, cbc_ref,
                dtc_ref, dtr_ref, bc_ref, br_ref, ac_ref, ar_ref, dsk_ref, h0_ref,
                y_ref, st_ref, xs_scr, btmp_scr, bmt_scr, cm_scr, ht_scr, *, n_ctx_blk, heads, hdim, seq,
                seq_chunks):
    q = CHUNK
    n = CONV_PIECE
    block_rows, rp = x_ref.shape
    n_chunks = block_rows // q
    n_seqs = n_chunks // seq_chunks
    is_ctx = pl.program_id(0) < n_ctx_blk
    seg = _seg_len(n_ctx_blk, seq)
    for r0 in range(0, block_rows, n):
        kp, kn = _conv3_masks(r0, seg)
        for c0 in range(0, rp, 128):
            xa = _silu(_conv3_unit(x_ref, wx_ref, cbx_ref, r0, c0, kp, kn))
            xs_scr[r0:r0 + n, c0:c0 + 128] = xa
            y_ref[r0:r0 + n, c0:c0 + 128] = xa * dsk_ref[:, c0:c0 + 128]
        btmp_scr[r0:r0 + n, :] = _silu(_conv3_unit(b_ref, wb_ref, cbb_ref, r0, 0, kp, kn))
        cm_scr[r0:r0 + n, :] = _silu(_conv3_unit(c_ref, wc_ref, cbc_ref, r0, 0, kp, kn)).astype(BF16)
    for c in range(n_chunks):
        bmt_scr[c] = btmp_scr[c * q:(c + 1) * q, :].T.astype(BF16)
    for d in range(2):
        ht_scr[d] = jnp.where(is_ctx, 0.0, h0_ref[d].T)

    row = lax.broadcasted_iota(jnp.int32, (q, q), 0)
    col = lax.broadcasted_iota(jnp.int32, (q, q), 1)
    masks = (row >= col, row <= col)
    tris = (masks[0].astype(F32), masks[1].astype(F32))
    lane = lax.broadcasted_iota(jnp.int32, (q, 128), 1)
    a_c = -jnp.exp(ac_ref[...])
    a_r = -jnp.exp(ar_ref[...])
    per = 128 // hdim

    def one_direction(ci, d, restart):
        rows = pl.ds(pl.multiple_of(ci * q, q), q)
        x = xs_scr[rows, :]
        cm = cm_scr[rows, :]
        bmt = bmt_scr[ci]
        scores = jnp.dot(cm, bmt, preferred_element_type=F32)
        dt_c = _softplus(dtc_ref[rows, :] + bc_ref[...])
        dt_r = _softplus(dtr_ref[ci] + br_ref[...])
        cum_c = jnp.dot(tris[d], dt_c * a_c, precision=lax.Precision.HIGHEST, preferred_element_type=F32)
        cum_r = jnp.dot(dt_r * a_r, tris[1 - d], precision=lax.Precision.HIGHEST, preferred_element_type=F32)
        end = q - 1 if d == 0 else 0
        tot_c = cum_c[end:end + 1, :]
        cum_i = jnp.dot(_split3(cum_c, 2 * heads), _head_selector(d, heads, q, heads * q),
                        preferred_element_type=F32)
        factors = jnp.concatenate([jnp.exp(cum_c), dt_c * jnp.exp(tot_c - cum_c)], axis=0)
        spread = jnp.dot(_split3(factors, 2 * heads), _head_selector(d, heads, hdim, rp),
                         preferred_element_type=F32)
        e_cum, w_end = spread[:q], spread[q:]
        h_in = ht_scr[d]
        if restart:
            h_in = jnp.where(is_ctx, 0.0, h_in)
        y_off = jnp.dot(cm, h_in.astype(BF16), preferred_element_type=F32) * e_cum
        y_blocks = []
        for g in range(heads // per):
            xg = x[:, g * 128:(g + 1) * 128]
            w_parts, x_parts = [], []
            for k in range(per):
                r = g * per + k
                rr = d * heads + r
                seg_ij = cum_i[:, r * q:(r + 1) * q] - cum_r[rr:rr + 1, :]
                w = scores * jnp.exp(jnp.where(masks[d], seg_ij, -jnp.inf)) * dt_r[rr:rr + 1, :]
                w_parts.append(w.astype(BF16))
                own = jnp.logical_and(lane >= k * hdim, lane < (k + 1) * hdim)
                x_parts.append(jnp.where(own, xg, 0.0).astype(BF16))
            y_blocks.append(jnp.dot(jnp.concatenate(w_parts, axis=1), jnp.concatenate(x_parts, axis=0),
                                    preferred_element_type=F32))
        y_ref[rows, :] += jnp.concatenate(y_blocks, axis=1) + y_off
        st = jnp.dot(bmt, (x * w_end).astype(BF16), preferred_element_type=F32)
        ht_scr[d] = e_cum[end:end + 1, :] * h_in + st

    def per_sequence(s, carry):
        for k in range(seq_chunks):
            ci = s * seq_chunks + k
            one_direction(ci, 0, k == 0)
            one_direction(n_chunks - 1 - ci, 1, k == 0)
        st_ref[s, 0] = ht_scr[0].T
        st_ref[n_seqs - 1 - s, 1] = ht_scr[1].T
        return carry

    lax.fori_loop(0, n_seqs, per_sequence, 0)


def _ssd(xd, conv_w, conv_b, dtc, dtr, dt_bias, a_log, d_skip, state_ssm, layer, *, d_ssm, groups, n_state,
         n_ctx_rows, seq, block_rows):
    t = xd.shape[0]
    two_r = dtc.shape[-1]
    heads = two_r // 2
    rp = d_ssm // groups
    hdim = rp // heads
    assert 128 % hdim == 0 and heads % (128 // hdim) == 0 and rp % 128 == 0
    n_ctx_blk = n_ctx_rows // block_rows
    seqs = block_rows // seq
    n_chunks = block_rows // CHUNK
    nb = d_ssm // n_state
    kw, d_xbc = conv_w.shape
    dec_b, depth = state_ssm.shape[:2]
    h0 = state_ssm.reshape(dec_b, depth, 2, d_ssm, n_state)
    bias = dt_bias.reshape(2, groups, heads).transpose(1, 0, 2).reshape(groups, 1, two_r)
    alog = a_log.reshape(2, groups, heads).transpose(1, 0, 2).reshape(groups, 1, two_r)
    bias3, alog3, dtc3 = (jnp.tile(v, (1, 1, 3)) for v in (bias, alog, dtc))
    dsk = jnp.repeat(d_skip, hdim).reshape(groups, 1, rp)
    cb = conv_b.reshape(1, d_xbc)
    kern = functools.partial(_ssd_kernel, n_ctx_blk=n_ctx_blk, heads=heads, hdim=hdim, seq=seq,
                             seq_chunks=seq // CHUNK)
    return pl.pallas_call(
        kern,
        grid=(t // block_rows, groups),
        in_specs=[pl.BlockSpec((block_rows, rp), lambda b, g: (b, g)),
                  pl.BlockSpec((block_rows, n_state), lambda b, g: (b, nb + g)),
                  pl.BlockSpec((block_rows, n_state), lambda b, g: (b, nb + groups + g)),
                  pl.BlockSpec((kw, rp), lambda b, g: (0, g)),
                  pl.BlockSpec((kw, n_state), lambda b, g: (0, nb + g)),
                  pl.BlockSpec((kw, n_state), lambda b, g: (0, nb + groups + g)),
                  pl.BlockSpec((1, rp), lambda b, g: (0, g)),
                  pl.BlockSpec((1, n_state), lambda b, g: (0, nb + g)),
                  pl.BlockSpec((1, n_state), lambda b, g: (0, nb + groups + g)),
                  pl.BlockSpec((None, block_rows, 3 * two_r), lambda b, g: (g, b, 0)),
                  pl.BlockSpec((None, n_chunks, two_r, CHUNK), lambda b, g: (g, b, 0, 0)),
                  pl.BlockSpec((None, 1, 3 * two_r), lambda b, g: (g, 0, 0)),
                  pl.BlockSpec((None, two_r, 1), lambda b, g: (g, 0, 0)),
                  pl.BlockSpec((None, 1, 3 * two_r), lambda b, g: (g, 0, 0)),
                  pl.BlockSpec((None, two_r, 1), lambda b, g: (g, 0, 0)),
                  pl.BlockSpec((None, 1, rp), lambda b, g: (g, 0, 0)),
                  pl.BlockSpec((None, None, 2, rp, n_state),
                               lambda b, g: (jnp.maximum(b - n_ctx_blk, 0), layer, 0, g, 0))],
        out_specs=[pl.BlockSpec((block_rows, rp), lambda b, g: (b, g)),
                   pl.BlockSpec((seqs, 2, rp, n_state), lambda b, g: (b, 0, g, 0))],
        out_shape=[jax.ShapeDtypeStruct((t, d_ssm), F32),
                   jax.ShapeDtypeStruct((t // seq, 2, d_ssm, n_state), F32)],
        scratch_shapes=[pltpu.VMEM((block_rows, rp), F32), pltpu.VMEM((block_rows, n_state), F32),
                        pltpu.VMEM((n_chunks, n_state, CHUNK), BF16), pltpu.VMEM((block_rows, n_state), BF16),
                        pltpu.VMEM((2, n_state, rp), F32)],
        compiler_params=_params("parallel", "parallel"),
    )(xd, xd, xd, conv_w, conv_w, conv_w, cb, cb, cb, dtc3, dtr, bias3, bias.reshape(groups, two_r, 1),
      alog3, alog.reshape(groups, two_r, 1), dsk, h0)


def kernel(x_prompt, x_sample, state_ssm, c, c_ctx, w_mod, b_mod, g_pre_mix, g_post_mix, g_pre_ffn, g_post_ffn, w_in, ssm_conv_w, ssm_conv_b, dt_bias, a_log, d_skip, g_ssm_norm, w_br_ssm, cv_conv_w, cv_conv_b, cv_ln_g, cv_ln_b, w_cv_out, w_out, w_up, ffn_conv_w, ffn_conv_b, w_down):
    batch, seq, d = x_prompt.shape
    dec_b, dec_seq, _ = x_sample.shape
    depth = w_mod.shape[0]
    n_state = state_ssm.shape[-1]
    heads_all = dt_bias.shape[-1]
    d_ssm = g_ssm_norm.shape[-1]
    d_xbc = ssm_conv_w.shape[-1]
    d_conv = cv_conv_w.shape[-1]
    groups = (d_xbc - d_ssm) // (2 * n_state)
    heads = heads_all // groups
    n_ctx_rows = batch * seq
    t = n_ctx_rows + dec_b * dec_seq
    assert seq % ROW_BLOCK == 0 and dec_seq % seq == 0 and dec_seq % GRID_W == 0 and ROW_BLOCK % GRID_W == 0
    assert n_ctx_rows % dec_seq == 0 and 1 + dec_b <= MOD_ROWS
    assert seq & (seq - 1) == 0 and GRID_W & (GRID_W - 1) == 0
    geo = (n_ctx_rows // ROW_BLOCK, dec_seq // ROW_BLOCK)
    z_end = d_ssm
    dt_end = z_end + d_xbc + 2 * heads_all

    x = jnp.concatenate([x_prompt.reshape(n_ctx_rows, d), x_sample.reshape(dec_b * dec_seq, d)], axis=0)
    cvec = jnp.concatenate([c_ctx[None, :], c, jnp.zeros((MOD_ROWS - 1 - dec_b, d), F32)], axis=0)
    mods = _modulation(cvec, w_mod, b_mod).reshape(depth, MOD_ROWS * 6, 1, d)

    h = _prenorm(x, g_pre_mix[0], mods[0], 1, 0, geo)
    states = []
    for l in range(depth):
        w_z = w_in[l, :, :z_end].astype(BF16)
        w_xd = w_in[l, :, z_end:dt_end].astype(BF16)
        w_gg = w_in[l, :, dt_end:].astype(BF16)
        z = _matmul(h, w_z, tm=1024, tn=512)
        xd = _matmul(h, w_xd, tm=1024, tn=896)
        gg = _matmul(h, w_gg, tm=1024, tn=512)

        dt_raw = xd[:, d_xbc:].reshape(t, 2, groups, heads)
        dtc = dt_raw.transpose(2, 0, 1, 3).reshape(groups, t, 2 * heads)
        dtr = dt_raw.transpose(2, 1, 3, 0).reshape(groups, 2 * heads, t // CHUNK, CHUNK).transpose(0, 2, 1, 3)
        y, st = _ssd(xd, ssm_conv_w[l], ssm_conv_b[l], dtc, dtr, dt_bias[l], a_log[l], d_skip[l], state_ssm, l,
                     d_ssm=d_ssm, groups=groups, n_state=n_state, n_ctx_rows=n_ctx_rows, seq=seq,
                     block_rows=dec_seq)
        states.append(st[:batch])
        yn = _gated_norm(y, z, g_ssm_norm[l])

        u = _glu_conv(gg, cv_conv_w[l], cv_conv_b[l], n_ctx_rows, seq, tc=512)
        ucv = _ln_silu(u, cv_ln_g[l], cv_ln_b[l])

        merged = _merge(yn, ucv, w_br_ssm[l].astype(BF16), w_cv_out[l].astype(BF16), gg, 2 * d_conv,
                        tm=512, tn=512)
        m = _matmul(merged, w_out[l].astype(BF16), tm=1024, tn=512)
        x, h = _resnorm(x, m, g_post_mix[l], mods[l], 2, g_pre_ffn[l], mods[l], 4, 3, geo)

        up = _matmul(h, w_up[l].astype(BF16), tm=1024, tn=512)
        act = _ffn_act(up, ffn_conv_w[l], ffn_conv_b[l], n_ctx_rows, seq, tr=1024, tc=256)
        f = _matmul(act, w_down[l].astype(BF16), tm=512, tn=512)
        if l + 1 < depth:
            x, h = _resnorm(x, f, g_post_ffn[l], mods[l], 5, g_pre_mix[l + 1], mods[l + 1], 1, 0, geo)
        else:
            x = _res(x, f, g_post_ffn[l], mods[l], 5, geo)

    y_prompt = x[:n_ctx_rows].reshape(batch, seq, d)
    y_sample = x[n_ctx_rows:].reshape(dec_b, dec_seq, d)
    new_state = jnp.stack(states, axis=1).reshape(batch, depth, 2, heads_all, d_ssm // heads_all, n_state)
    return (y_prompt, y_sample, new_state)
```

```python
import functools
import itertools

import jax
import jax.numpy as jnp
from jax import lax
from jax.experimental import pallas as pl
from jax.experimental.pallas import tpu as pltpu

GRID_W = 64
CHUNK = 128
EPS = 1e-6
ROW_BLOCK = 256
CONV_PAD = 16
CONV_PIECE = 64
LOCKSTEP_CHUNKS = 4
MOD_ROWS = 16
V7X_VMEM_LIMIT = 56 * 1024 * 1024

F32 = jnp.float32
BF16 = jnp.bfloat16


def _params(*sem):
    return pltpu.CompilerParams(dimension_semantics=sem, vmem_limit_bytes=V7X_VMEM_LIMIT)


def _tile(n, pref):
    if n <= pref:
        return n
    t = (pref // 128) * 128
    while n % t:
        t -= 128
    return t


def _sigmoid(x):
    return 1.0 / (1.0 + jnp.exp(-x))


def _silu(x):
    return x * _sigmoid(x)


def _softplus(x):
    return jnp.maximum(x, 0.0) + jnp.log1p(jnp.exp(-jnp.abs(x)))


def _rms(x, g):
    return x * lax.rsqrt(jnp.mean(x * x, axis=-1, keepdims=True) + EPS) * g


def _mm_kernel(x_ref, w_ref, o_ref):
    o_ref[...] = jnp.dot(x_ref[...], w_ref[...], preferred_element_type=F32).astype(o_ref.dtype)


def _matmul(x, w, *, tm, tn, out_dtype=F32):
    m, k = x.shape
    n = w.shape[1]
    tm, tn = _tile(m, tm), _tile(n, tn)
    return pl.pallas_call(
        _mm_kernel,
        grid=(m // tm, n // tn),
        in_specs=[pl.BlockSpec((tm, k), lambda i, j: (i, 0)),
                  pl.BlockSpec((k, tn), lambda i, j: (0, j))],
        out_specs=pl.BlockSpec((tm, tn), lambda i, j: (i, j)),
        out_shape=jax.ShapeDtypeStruct((m, n), out_dtype),
        compiler_params=_params("parallel", "parallel"),
    )(x, w)


def _mm_wcast_kernel(x_ref, w_ref, o_ref, wbf_scr):
    @pl.when(pl.program_id(1) == 0)
    def _():
        wbf_scr[...] = w_ref[0].astype(BF16)

    o_ref[...] = jnp.dot(x_ref[...], wbf_scr[...], preferred_element_type=F32).astype(o_ref.dtype)


def _matmul_wcast(x, w_all, layer, col0, ncols, *, tm, tn, out_dtype=F32):
    m, k = x.shape
    if col0 % 128 or ncols % 128:
        return _matmul(x, w_all[layer, :, col0:col0 + ncols].astype(BF16), tm=tm, tn=tn, out_dtype=out_dtype)
    tm, tn = _tile(m, tm), _tile(ncols, tn)
    return pl.pallas_call(
        _mm_wcast_kernel,
        grid=(ncols // tn, m // tm),
        in_specs=[pl.BlockSpec((tm, k), lambda j, i: (i, 0)),
                  pl.BlockSpec((pl.Element(1), pl.Element(k), pl.Element(tn)),
                               lambda j, i: (layer, 0, pl.multiple_of(col0 + j * tn, 128)))],
        out_specs=pl.BlockSpec((tm, tn), lambda j, i: (i, j)),
        out_shape=jax.ShapeDtypeStruct((m, ncols), out_dtype),
        scratch_shapes=[pltpu.VMEM((k, tn), BF16)],
        compiler_params=_params("parallel", "arbitrary"),
    )(x, w_all)


def _merge_kernel(y_ref, u_ref, w1_ref, w2_ref, g1_ref, g2_ref, o_ref):
    a = jnp.dot(y_ref[...], w1_ref[...], preferred_element_type=F32)
    b = jnp.dot(u_ref[...], w2_ref[...], preferred_element_type=F32)
    o_ref[...] = (_sigmoid(g1_ref[...]) * a + _sigmoid(g2_ref[...]) * b).astype(o_ref.dtype)


def _merge(yn, ucv, w1, w2, gg, gate_col0, *, tm, tn):
    m, k = yn.shape
    n = w1.shape[1]
    tm, tn = _tile(m, tm), _tile(n, tn)
    c1 = gate_col0 // tn
    c2 = (gate_col0 + n) // tn
    return pl.pallas_call(
        _merge_kernel,
        grid=(m // tm, n // tn),
        in_specs=[pl.BlockSpec((tm, k), lambda i, j: (i, 0)),
                  pl.BlockSpec((tm, k), lambda i, j: (i, 0)),
                  pl.BlockSpec((k, tn), lambda i, j: (0, j)),
                  pl.BlockSpec((k, tn), lambda i, j: (0, j)),
                  pl.BlockSpec((tm, tn), lambda i, j: (i, c1 + j)),
                  pl.BlockSpec((tm, tn), lambda i, j: (i, c2 + j))],
        out_specs=pl.BlockSpec((tm, tn), lambda i, j: (i, j)),
        out_shape=jax.ShapeDtypeStruct((m, n), BF16),
        compiler_params=_params("parallel", "parallel"),
    )(yn, ucv, w1, w2, gg, gg)


def _mod_kernel(c_ref, w_ref, b_ref, o_ref):
    cs = _silu(c_ref[...]).astype(BF16)
    o_ref[...] = jnp.dot(cs, w_ref[...].astype(BF16), preferred_element_type=F32) + b_ref[...]


def _modulation(cvec, w_mod, b_mod):
    depth, d, n = w_mod.shape
    tn = _tile(n, 512)
    return pl.pallas_call(
        _mod_kernel,
        grid=(depth, n // tn),
        in_specs=[pl.BlockSpec((MOD_ROWS, d), lambda l, j: (0, 0)),
                  pl.BlockSpec((None, d, tn), lambda l, j: (l, 0, j)),
                  pl.BlockSpec((None, 1, tn), lambda l, j: (l, 0, j))],
        out_specs=pl.BlockSpec((None, MOD_ROWS, tn), lambda l, j: (l, 0, j)),
        out_shape=jax.ShapeDtypeStruct((depth, MOD_ROWS, n), F32),
        compiler_params=_params("parallel", "parallel"),
    )(cvec, w_mod, b_mod.reshape(depth, 1, n))


def _prenorm_kernel(x_ref, g_ref, sc_ref, sh_ref, h_ref):
    h_ref[...] = (_rms(x_ref[...], g_ref[...]) * (1.0 + sc_ref[...]) + sh_ref[...]).astype(h_ref.dtype)


def _resnorm_kernel(x_ref, r_ref, ga_ref, gt_ref, gb_ref, sc_ref, sh_ref, xo_ref, h_ref):
    xn = x_ref[...] + gt_ref[...] * _rms(r_ref[...], ga_ref[...])
    xo_ref[...] = xn
    h_ref[...] = (_rms(xn, gb_ref[...]) * (1.0 + sc_ref[...]) + sh_ref[...]).astype(h_ref.dtype)


def _res_kernel(x_ref, r_ref, ga_ref, gt_ref, xo_ref):
    xo_ref[...] = x_ref[...] + gt_ref[...] * _rms(r_ref[...], ga_ref[...])


def _mod_spec(d, which, n_ctx_blk, lat_blk_per_req):
    def index(i):
        row = (i >= n_ctx_blk).astype(jnp.int32) * (1 + (i - n_ctx_blk) // lat_blk_per_req)
        return (row * 6 + which, 0, 0)
    return pl.BlockSpec((None, 1, d), index)


def _row_spec(d):
    return pl.BlockSpec((ROW_BLOCK, d), lambda i: (i, 0))


def _vec_spec(d):
    return pl.BlockSpec((1, d), lambda i: (0, 0))


def _prenorm(x, g, mods, sc_idx, sh_idx, geo):
    t, d = x.shape
    ms = functools.partial(_mod_spec, d, n_ctx_blk=geo[0], lat_blk_per_req=geo[1])
    return pl.pallas_call(
        _prenorm_kernel,
        grid=(t // ROW_BLOCK,),
        in_specs=[_row_spec(d), _vec_spec(d), ms(sc_idx), ms(sh_idx)],
        out_specs=_row_spec(d),
        out_shape=jax.ShapeDtypeStruct((t, d), BF16),
        compiler_params=_params("parallel"),
    )(x, g.reshape(1, d), mods, mods)


def _resnorm(x, r, ga, mods_gate, gt_idx, gb, mods_next, sc_idx, sh_idx, geo):
    t, d = x.shape
    ms = functools.partial(_mod_spec, d, n_ctx_blk=geo[0], lat_blk_per_req=geo[1])
    return pl.pallas_call(
        _resnorm_kernel,
        grid=(t // ROW_BLOCK,),
        in_specs=[_row_spec(d), _row_spec(d), _vec_spec(d), ms(gt_idx), _vec_spec(d), ms(sc_idx), ms(sh_idx)],
        out_specs=[_row_spec(d), _row_spec(d)],
        out_shape=[jax.ShapeDtypeStruct((t, d), F32), jax.ShapeDtypeStruct((t, d), BF16)],
        compiler_params=_params("parallel"),
    )(x, r, ga.reshape(1, d), mods_gate, gb.reshape(1, d), mods_next, mods_next)


def _res(x, r, ga, mods_gate, gt_idx, geo):
    t, d = x.shape
    ms = functools.partial(_mod_spec, d, n_ctx_blk=geo[0], lat_blk_per_req=geo[1])
    return pl.pallas_call(
        _res_kernel,
        grid=(t // ROW_BLOCK,),
        in_specs=[_row_spec(d), _row_spec(d), _vec_spec(d), ms(gt_idx)],
        out_specs=_row_spec(d),
        out_shape=jax.ShapeDtypeStruct((t, d), F32),
        compiler_params=_params("parallel"),
    )(x, r, ga.reshape(1, d), mods_gate)


def _gated_norm_kernel(y_ref, z_ref, g_ref, o_ref):
    o_ref[...] = _rms(y_ref[...] * _silu(z_ref[...]), g_ref[...]).astype(o_ref.dtype)


def _gated_norm(y, z, g):
    t, d = y.shape
    return pl.pallas_call(
        _gated_norm_kernel,
        grid=(t // ROW_BLOCK,),
        in_specs=[_row_spec(d), _row_spec(d), _vec_spec(d)],
        out_specs=_row_spec(d),
        out_shape=jax.ShapeDtypeStruct((t, d), BF16),
        compiler_params=_params("parallel"),
    )(y, z, g.reshape(1, d))


def _ln_silu_kernel(u_ref, g_ref, b_ref, o_ref):
    u = u_ref[...]
    uc = u - jnp.mean(u, axis=-1, keepdims=True)
    var = jnp.mean(uc * uc, axis=-1, keepdims=True)
    o_ref[...] = _silu(uc * lax.rsqrt(var + EPS) * g_ref[...] + b_ref[...]).astype(o_ref.dtype)


def _ln_silu(u, g, b):
    t, d = u.shape
    return pl.pallas_call(
        _ln_silu_kernel,
        grid=(t // ROW_BLOCK,),
        in_specs=[_row_spec(d), _vec_spec(d), _vec_spec(d)],
        out_specs=_row_spec(d),
        out_shape=jax.ShapeDtypeStruct((t, d), BF16),
        compiler_params=_params("parallel"),
    )(u, g.reshape(1, d), b.reshape(1, d))


def _seg_len(n_ctx_blk, seq):
    return jnp.where(pl.program_id(0) < n_ctx_blk, seq, GRID_W)


def _conv3_masks(r0, seg):
    rows = lax.broadcasted_iota(jnp.int32, (8, 128), 0) + r0
    return (rows & (seg - 1)) != 0, ((rows + CONV_PIECE - 8) & (seg - 1)) != seg - 1


def _conv3_unit(x_ref, w_ref, b_ref, r0, c0, keep_prev, keep_next):
    n = CONV_PIECE
    cs = slice(c0, c0 + 128)
    x = x_ref[r0:r0 + n, cs]
    prev = x_ref[r0 - 1:r0 + n - 1, cs] if r0 > 0 else pltpu.roll(x, 1, 0)
    nxt = x_ref[r0 + 1:r0 + n + 1, cs] if r0 + n < x_ref.shape[0] else pltpu.roll(x, n - 1, 0)
    prev = jnp.concatenate([jnp.where(keep_prev, prev[:8], 0.0), prev[8:]], axis=0)
    nxt = jnp.concatenate([nxt[:n - 8], jnp.where(keep_next, nxt[n - 8:], 0.0)], axis=0)
    acc = w_ref[1:2, cs] * x + b_ref[:, cs]
    acc = acc + w_ref[0:1, cs] * prev
    return acc + w_ref[2:3, cs] * nxt


def _ffn_act_kernel(ug_ref, uv_ref, wg_ref, wv_ref, bg_ref, bv_ref, o_ref, *, n_ctx_blk, seq):
    seg = _seg_len(n_ctx_blk, seq)
    rows, cols = o_ref.shape
    for r0 in range(0, rows, CONV_PIECE):
        kp, kn = _conv3_masks(r0, seg)
        for c0 in range(0, cols, 128):
            gate = _conv3_unit(ug_ref, wg_ref, bg_ref, r0, c0, kp, kn)
            val = _conv3_unit(uv_ref, wv_ref, bv_ref, r0, c0, kp, kn)
            o_ref[r0:r0 + CONV_PIECE, c0:c0 + 128] = (_silu(gate) * val).astype(o_ref.dtype)


def _ffn_act(u, w, b, n_ctx_rows, seq, *, tr, tc):
    t, f2 = u.shape
    f = f2 // 2
    tr, tc = _tile(n_ctx_rows, tr), _tile(f, tc)
    nj = f // tc
    kw = w.shape[0]
    b2 = b.reshape(1, f2)
    return pl.pallas_call(
        functools.partial(_ffn_act_kernel, n_ctx_blk=n_ctx_rows // tr, seq=seq),
        grid=(t // tr, nj),
        in_specs=[pl.BlockSpec((tr, tc), lambda i, j: (i, j)),
                  pl.BlockSpec((tr, tc), lambda i, j: (i, nj + j)),
                  pl.BlockSpec((kw, tc), lambda i, j: (0, j)),
                  pl.BlockSpec((kw, tc), lambda i, j: (0, nj + j)),
                  pl.BlockSpec((1, tc), lambda i, j: (0, j)),
                  pl.BlockSpec((1, tc), lambda i, j: (0, nj + j))],
        out_specs=pl.BlockSpec((tr, tc), lambda i, j: (i, j)),
        out_shape=jax.ShapeDtypeStruct((t, f), BF16),
        compiler_params=_params("parallel", "parallel"),
    )(u, u, w, w, b2, b2)


def _glu_conv_kernel(ua_ref, ug_ref, w_ref, b_ref, o_ref, pad_ref, *, n_ctx_blk, seq, kw):
    rows, cols = o_ref.shape
    half = kw // 2
    piece = min(GRID_W, rows)

    def run(seg):
        stride = seg + 2 * CONV_PAD
        used = (rows // seg) * stride
        zeros = jnp.zeros((CONV_PAD, cols), F32)
        for s in range(rows // seg):
            base = s * stride
            src = slice(s * seg, (s + 1) * seg)
            pad_ref[0, base:base + CONV_PAD, :] = zeros
            pad_ref[0, base + CONV_PAD:base + CONV_PAD + seg, :] = ua_ref[src, :] * _sigmoid(ug_ref[src, :])
            pad_ref[0, base + CONV_PAD + seg:base + stride, :] = zeros
        for r in range(1, 8):
            pad_ref[r, 0:used - 8, :] = pad_ref[0, r:used - 8 + r, :]
        for s in range(rows // seg):
            for p in range(seg // piece):
                a = s * stride + p * piece
                acc = jnp.zeros((piece, cols), F32)
                for k in range(kw):
                    off = CONV_PAD - half + k
                    acc = acc + w_ref[k:k + 1, :] * pad_ref[off % 8, a + off - off % 8:a + off - off % 8 + piece, :]
                out0 = s * seg + p * piece
                o_ref[out0:out0 + piece, :] = acc + b_ref[...]

    is_ctx = pl.program_id(0) < n_ctx_blk

    @pl.when(is_ctx)
    def _():
        run(min(seq, rows))

    @pl.when(jnp.logical_not(is_ctx))
    def _():
        run(GRID_W)


def _glu_conv(gg, w, b, n_ctx_rows, seq, *, tc):
    t = gg.shape[0]
    kw, c = w.shape
    assert kw // 2 <= CONV_PAD
    tr = ROW_BLOCK
    tc = _tile(c, tc)
    nj = c // tc
    pad_rows = (tr // GRID_W) * (GRID_W + 2 * CONV_PAD)
    return pl.pallas_call(
        functools.partial(_glu_conv_kernel, n_ctx_blk=n_ctx_rows // tr, seq=seq, kw=kw),
        grid=(t // tr, nj),
        in_specs=[pl.BlockSpec((tr, tc), lambda i, j: (i, j)),
                  pl.BlockSpec((tr, tc), lambda i, j: (i, nj + j)),
                  pl.BlockSpec((kw, tc), lambda i, j: (0, j)),
                  pl.BlockSpec((1, tc), lambda i, j: (0, j))],
        out_specs=pl.BlockSpec((tr, tc), lambda i, j: (i, j)),
        out_shape=jax.ShapeDtypeStruct((t, c), F32),
        scratch_shapes=[pltpu.VMEM((8, pad_rows, tc), F32)],
        compiler_params=_params("parallel", "parallel"),
    )(gg, gg, w, b.reshape(1, c))


def _spread_selector(which, heads, reps, rows):
    assert reps & (reps - 1) == 0
    out_lanes = heads * reps
    k = lax.broadcasted_iota(jnp.int32, (rows, out_lanes), 0)
    head = lax.shift_right_logical(lax.broadcasted_iota(jnp.int32, (rows, out_lanes), 1), reps.bit_length() - 1)
    hit = k == head + which * heads
    for piece in (1, 2):
        hit = jnp.logical_or(hit, k == head + (piece * 3 + which) * heads)
    return hit.astype(BF16)


def _ssd_kernel(x_ref, b_ref, c_ref, wx_ref, wb_ref, wc_ref, cbx_ref, cbb_ref, cbc_ref,
                dtr_ref, br_ref, ar_ref, dsk_ref, h0_ref,
                y_ref, st_ref, xs_scr, btmp_scr, bmt_scr, cm_scr, ht_scr, sela_scr, selb_scr, ec_scr, st_scr,
                *, n_ctx_blk, heads, hdim, seq, seq_chunks):
    q = CHUNK
    n = CONV_PIECE
    block_rows, rp = x_ref.shape
    n_chunks = block_rows // q
    n_seqs = n_chunks // seq_chunks
    is_ctx = pl.program_id(0) < n_ctx_blk
    seg = _seg_len(n_ctx_blk, seq)
    for r0 in range(0, block_rows, n):
        kp, kn = _conv3_masks(r0, seg)
        for c0 in range(0, rp, 128):
            xa = _silu(_conv3_unit(x_ref, wx_ref, cbx_ref, r0, c0, kp, kn))
            xs_scr[r0:r0 + n, c0:c0 + 128] = xa
            y_ref[r0:r0 + n, c0:c0 + 128] = xa * dsk_ref[:, c0:c0 + 128]
        btmp_scr[r0:r0 + n, :] = _silu(_conv3_unit(b_ref, wb_ref, cbb_ref, r0, 0, kp, kn))
        cm_scr[r0:r0 + n, :] = _silu(_conv3_unit(c_ref, wc_ref, cbc_ref, r0, 0, kp, kn)).astype(BF16)
    for c in range(n_chunks):
        bmt_scr[c] = btmp_scr[c * q:(c + 1) * q, :].T.astype(BF16)
    for d in range(2):
        ht_scr[d] = jnp.where(is_ctx, 0.0, h0_ref[d].T)
    sela_scr[...] = _spread_selector(0, heads, q, q)
    selb_scr[...] = jnp.concatenate([_spread_selector(1, heads, hdim, q), _spread_selector(2, heads, hdim, q)], axis=1)

    row = lax.broadcasted_iota(jnp.int32, (q, q), 0)
    col = lax.broadcasted_iota(jnp.int32, (q, q), 1)
    masks = (row >= col, row <= col)
    tris = (masks[0].astype(F32), masks[1].astype(F32))
    lane = lax.broadcasted_iota(jnp.int32, (q, 128), 1)
    a_r = -jnp.exp(ar_ref[...])
    per = 128 // hdim
    pad = jnp.zeros((q - 9 * heads, q), F32)

    def chunk_local(ci, d):
        rows = slice(ci * q, (ci + 1) * q)
        hs = slice(d * heads, (d + 1) * heads)
        dt = _softplus(dtr_ref[ci] + br_ref[...])[hs, :]
        cum = jnp.dot(dt * a_r[hs, :], tris[1 - d], precision=lax.Precision.HIGHEST, preferred_element_type=F32)
        bmt = bmt_scr[ci]
        scores = jnp.dot(cm_scr[rows, :], bmt, preferred_element_type=F32)
        yield
        end = q - 1 if d == 0 else 0
        cum_j = cum - jnp.log(dt)
        v = jnp.concatenate([cum, jnp.exp(cum), dt * jnp.exp(cum[:, end:end + 1] - cum)], axis=0)
        hi = v.astype(BF16).astype(F32)
        mid = (v - hi).astype(BF16).astype(F32)
        lo = (v - hi - mid).astype(BF16).astype(F32)
        vt = jnp.concatenate([hi, mid, lo, pad], axis=0).T.astype(BF16)
        yield
        cum_i = jnp.dot(vt, sela_scr[...], preferred_element_type=F32)
        spread = jnp.dot(vt, selb_scr[...], preferred_element_type=F32)
        yield
        x = xs_scr[rows, :]
        ec_scr[d, ci] = spread[:, :rp]
        st_scr[d, ci] = jnp.dot(bmt, (x * spread[:, rp:]).astype(BF16), preferred_element_type=F32)
        y_blocks = []
        for g in range(heads // per):
            xg = x[:, g * 128:(g + 1) * 128]
            w_parts, x_parts = [], []
            for k in range(per):
                r = g * per + k
                seg_ij = cum_i[:, r * q:(r + 1) * q] - cum_j[r:r + 1, :]
                w = scores * jnp.exp(jnp.where(masks[d], seg_ij, -jnp.inf))
                w_parts.append(w.astype(BF16))
                own = jnp.logical_and(lane >= k * hdim, lane < (k + 1) * hdim)
                x_parts.append(jnp.where(own, xg, 0.0).astype(BF16))
            y_blocks.append(jnp.dot(jnp.concatenate(w_parts, axis=1), jnp.concatenate(x_parts, axis=0),
                                    preferred_element_type=F32))
            yield
        y_ref[rows, :] += jnp.concatenate(y_blocks, axis=1)

    for t in range(n_chunks // LOCKSTEP_CHUNKS):
        chains = [chunk_local(t * LOCKSTEP_CHUNKS + k, d) for k in range(LOCKSTEP_CHUNKS) for d in range(2)]
        for _ in itertools.zip_longest(*chains):
            pass

    def carry_state(ci, d, restart):
        rows = slice(ci * q, (ci + 1) * q)
        end = q - 1 if d == 0 else 0
        h_in = ht_scr[d]
        if restart:
            h_in = jnp.where(is_ctx, 0.0, h_in)
        e_cum = ec_scr[d, ci]
        y_ref[rows, :] += jnp.dot(cm_scr[rows, :], h_in.astype(BF16), preferred_element_type=F32) * e_cum
        ht_scr[d] = e_cum[end:end + 1, :] * h_in + st_scr[d, ci]

    for s in range(n_seqs):
        for k in range(seq_chunks):
            ci = s * seq_chunks + k
            carry_state(ci, 0, k == 0)
            carry_state(n_chunks - 1 - ci, 1, k == 0)
        st_ref[s, 0] = ht_scr[0].T
        st_ref[n_seqs - 1 - s, 1] = ht_scr[1].T


def _ssd(xd, conv_w, conv_b, dtr, dt_bias, a_log, d_skip, state_ssm, layer, *, d_ssm, groups, n_state,
         n_ctx_rows, seq, block_rows):
    t = xd.shape[0]
    two_r = dtr.shape[-2]
    heads = two_r // 2
    rp = d_ssm // groups
    hdim = rp // heads
    assert 128 % hdim == 0 and heads % (128 // hdim) == 0 and rp % 128 == 0 and 9 * heads <= CHUNK
    assert block_rows % (LOCKSTEP_CHUNKS * CHUNK) == 0
    n_ctx_blk = n_ctx_rows // block_rows
    seqs = block_rows // seq
    n_chunks = block_rows // CHUNK
    nb = d_ssm // n_state
    kw, d_xbc = conv_w.shape
    dec_b, depth = state_ssm.shape[:2]
    h0 = state_ssm.reshape(dec_b, depth, 2, d_ssm, n_state)
    bias = dt_bias.reshape(2, groups, heads).transpose(1, 0, 2).reshape(groups, two_r, 1)
    alog = a_log.reshape(2, groups, heads).transpose(1, 0, 2).reshape(groups, two_r, 1)
    dsk = jnp.repeat(d_skip, hdim).reshape(groups, 1, rp)
    cb = conv_b.reshape(1, d_xbc)
    kern = functools.partial(_ssd_kernel, n_ctx_blk=n_ctx_blk, heads=heads, hdim=hdim, seq=seq,
                             seq_chunks=seq // CHUNK)
    return pl.pallas_call(
        kern,
        grid=(t // block_rows, groups),
        in_specs=[pl.BlockSpec((block_rows, rp), lambda b, g: (b, g)),
                  pl.BlockSpec((block_rows, n_state), lambda b, g: (b, nb + g)),
                  pl.BlockSpec((block_rows, n_state), lambda b, g: (b, nb + groups + g)),
                  pl.BlockSpec((kw, rp), lambda b, g: (0, g)),
                  pl.BlockSpec((kw, n_state), lambda b, g: (0, nb + g)),
                  pl.BlockSpec((kw, n_state), lambda b, g: (0, nb + groups + g)),
                  pl.BlockSpec((1, rp), lambda b, g: (0, g)),
                  pl.BlockSpec((1, n_state), lambda b, g: (0, nb + g)),
                  pl.BlockSpec((1, n_state), lambda b, g: (0, nb + groups + g)),
                  pl.BlockSpec((None, n_chunks, two_r, CHUNK), lambda b, g: (g, b, 0, 0)),
                  pl.BlockSpec((None, two_r, 1), lambda b, g: (g, 0, 0)),
                  pl.BlockSpec((None, two_r, 1), lambda b, g: (g, 0, 0)),
                  pl.BlockSpec((None, 1, rp), lambda b, g: (g, 0, 0)),
                  pl.BlockSpec((None, None, 2, rp, n_state),
                               lambda b, g: (jnp.maximum(b - n_ctx_blk, 0), layer, 0, g, 0))],
        out_specs=[pl.BlockSpec((block_rows, rp), lambda b, g: (b, g)),
                   pl.BlockSpec((seqs, 2, rp, n_state), lambda b, g: (b, 0, g, 0))],
        out_shape=[jax.ShapeDtypeStruct((t, d_ssm), F32),
                   jax.ShapeDtypeStruct((t // seq, 2, d_ssm, n_state), F32)],
        scratch_shapes=[pltpu.VMEM((block_rows, rp), F32), pltpu.VMEM((block_rows, n_state), F32),
                        pltpu.VMEM((n_chunks, n_state, CHUNK), BF16), pltpu.VMEM((block_rows, n_state), BF16),
                        pltpu.VMEM((2, n_state, rp), F32),
                        pltpu.VMEM((CHUNK, heads * CHUNK), BF16), pltpu.VMEM((CHUNK, 2 * rp), BF16),
                        pltpu.VMEM((2, n_chunks, CHUNK, rp), F32), pltpu.VMEM((2, n_chunks, n_state, rp), F32)],
        compiler_params=_params("parallel", "parallel"),
    )(xd, xd, xd, conv_w, conv_w, conv_w, cb, cb, cb, dtr, bias, alog, dsk, h0)


def kernel(x_prompt, x_sample, state_ssm, c, c_ctx, w_mod, b_mod, g_pre_mix, g_post_mix, g_pre_ffn, g_post_ffn, w_in, ssm_conv_w, ssm_conv_b, dt_bias, a_log, d_skip, g_ssm_norm, w_br_ssm, cv_conv_w, cv_conv_b, cv_ln_g, cv_ln_b, w_cv_out, w_out, w_up, ffn_conv_w, ffn_conv_b, w_down):
    batch, seq, d = x_prompt.shape
    dec_b, dec_seq, _ = x_sample.shape
    depth = w_mod.shape[0]
    n_state = state_ssm.shape[-1]
    heads_all = dt_bias.shape[-1]
    d_ssm = g_ssm_norm.shape[-1]
    d_xbc = ssm_conv_w.shape[-1]
    d_conv = cv_conv_w.shape[-1]
    groups = (d_xbc - d_ssm) // (2 * n_state)
    heads = heads_all // groups
    n_ctx_rows = batch * seq
    t = n_ctx_rows + dec_b * dec_seq
    assert seq % ROW_BLOCK == 0 and dec_seq % seq == 0 and dec_seq % GRID_W == 0 and ROW_BLOCK % GRID_W == 0
    assert n_ctx_rows % dec_seq == 0 and 1 + dec_b <= MOD_ROWS
    assert seq & (seq - 1) == 0 and GRID_W & (GRID_W - 1) == 0 and GRID_W % CONV_PIECE == 0
    geo = (n_ctx_rows // ROW_BLOCK, dec_seq // ROW_BLOCK)
    z_end = d_ssm
    dt_end = z_end + d_xbc + 2 * heads_all
    in_cols = w_in.shape[-1]

    x = jnp.concatenate([x_prompt.reshape(n_ctx_rows, d), x_sample.reshape(dec_b * dec_seq, d)], axis=0)
    cvec = jnp.concatenate([c_ctx[None, :], c, jnp.zeros((MOD_ROWS - 1 - dec_b, d), F32)], axis=0)
    mods = _modulation(cvec, w_mod, b_mod).reshape(depth, MOD_ROWS * 6, 1, d)

    h = _prenorm(x, g_pre_mix[0], mods[0], 1, 0, geo)
    states = []
    for l in range(depth):
        z = _matmul_wcast(h, w_in, l, 0, z_end, tm=512, tn=1024)
        xd = _matmul_wcast(h, w_in, l, z_end, dt_end - z_end, tm=512, tn=896)
        gg = _matmul_wcast(h, w_in, l, dt_end, in_cols - dt_end, tm=512, tn=1024)

        dt_raw = xd[:, d_xbc:].reshape(t, 2, groups, heads)
        dtr = dt_raw.transpose(2, 1, 3, 0).reshape(groups, 2 * heads, t // CHUNK, CHUNK).transpose(0, 2, 1, 3)
        y, st = _ssd(xd, ssm_conv_w[l], ssm_conv_b[l], dtr, dt_bias[l], a_log[l], d_skip[l], state_ssm, l,
                     d_ssm=d_ssm, groups=groups, n_state=n_state, n_ctx_rows=n_ctx_rows, seq=seq,
                     block_rows=dec_seq)
        states.append(st[:batch])
        yn = _gated_norm(y, z, g_ssm_norm[l])

        u = _glu_conv(gg, cv_conv_w[l], cv_conv_b[l], n_ctx_rows, seq, tc=512)
        ucv = _ln_silu(u, cv_ln_g[l], cv_ln_b[l])

        merged = _merge(yn, ucv, w_br_ssm[l].astype(BF16), w_cv_out[l].astype(BF16), gg, 2 * d_conv,
                        tm=512, tn=512)
        m = _matmul(merged, w_out[l].astype(BF16), tm=1024, tn=512)
        x, h = _resnorm(x, m, g_post_mix[l], mods[l], 2, g_pre_ffn[l], mods[l], 4, 3, geo)

        up = _matmul(h, w_up[l].astype(BF16), tm=1024, tn=512)
        act = _ffn_act(up, ffn_conv_w[l], ffn_conv_b[l], n_ctx_rows, seq, tr=256, tc=5504)
        f = _matmul(act, w_down[l].astype(BF16), tm=512, tn=512)
        if l + 1 < depth:
            x, h = _resnorm(x, f, g_post_ffn[l], mods[l], 5, g_pre_mix[l + 1], mods[l + 1], 1, 0, geo)
        else:
            x = _res(x, f, g_post_ffn[l], mods[l], 5, geo)

    y_prompt = x[:n_ctx_rows].reshape(batch, seq, d)
    y_sample = x[n_ctx_rows:].reshape(dec_b, dec_seq, d)
    new_state = jnp.stack(states, axis=1).reshape(batch, depth, 2, heads_all, d_ssm // heads_all, n_state)
    return (y_prompt, y_sample, new_state)
```

```python
import functools
import itertools

import jax
import jax.numpy as jnp
from jax import lax
from jax.experimental import pallas as pl
from jax.experimental.pallas import tpu as pltpu

GRID_W = 64
CHUNK = 128
EPS = 1e-6
ROW_BLOCK = 256
CONV_PAD = 16
CONV_PIECE = 64
LOCKSTEP_CHUNKS = 4
MOD_ROWS = 16
V7X_VMEM_LIMIT = 56 * 1024 * 1024

F32 = jnp.float32
BF16 = jnp.bfloat16


def _params(*sem):
    return pltpu.CompilerParams(dimension_semantics=sem, vmem_limit_bytes=V7X_VMEM_LIMIT)


def _tile(n, pref):
    if n <= pref:
        return n
    t = (pref // 128) * 128
    while n % t:
        t -= 128
    return t


def _sigmoid(x):
    return 1.0 / (1.0 + jnp.exp(-x))


def _silu(x):
    return x * _sigmoid(x)


def _softplus(x):
    return jnp.maximum(x, 0.0) + jnp.log1p(jnp.exp(-jnp.abs(x)))


def _rms(x, g):
    return x * lax.rsqrt(jnp.mean(x * x, axis=-1, keepdims=True) + EPS) * g


def _mm_kernel(x_ref, w_ref, o_ref):
    o_ref[...] = jnp.dot(x_ref[...], w_ref[...], preferred_element_type=F32).astype(o_ref.dtype)


def _matmul(x, w, *, tm, tn, out_dtype=F32):
    m, k = x.shape
    n = w.shape[1]
    tm, tn = _tile(m, tm), _tile(n, tn)
    return pl.pallas_call(
        _mm_kernel,
        grid=(m // tm, n // tn),
        in_specs=[pl.BlockSpec((tm, k), lambda i, j: (i, 0)),
                  pl.BlockSpec((k, tn), lambda i, j: (0, j))],
        out_specs=pl.BlockSpec((tm, tn), lambda i, j: (i, j)),
        out_shape=jax.ShapeDtypeStruct((m, n), out_dtype),
        compiler_params=_params("parallel", "parallel"),
    )(x, w)


def _mm_wcast_kernel(x_ref, w_ref, o_ref, wbf_scr):
    @pl.when(pl.program_id(1) == 0)
    def _():
        wbf_scr[...] = w_ref[0].astype(BF16)

    o_ref[...] = jnp.dot(x_ref[...], wbf_scr[...], preferred_element_type=F32).astype(o_ref.dtype)


def _matmul_wcast(x, w_all, layer, col0, ncols, *, tm, tn, out_dtype=F32):
    m, k = x.shape
    if col0 % 128 or ncols % 128:
        return _matmul(x, w_all[layer, :, col0:col0 + ncols].astype(BF16), tm=tm, tn=tn, out_dtype=out_dtype)
    tm, tn = _tile(m, tm), _tile(ncols, tn)
    return pl.pallas_call(
        _mm_wcast_kernel,
        grid=(ncols // tn, m // tm),
        in_specs=[pl.BlockSpec((tm, k), lambda j, i: (i, 0)),
                  pl.BlockSpec((pl.Element(1), pl.Element(k), pl.Element(tn)),
                               lambda j, i: (layer, 0, pl.multiple_of(col0 + j * tn, 128)))],
        out_specs=pl.BlockSpec((tm, tn), lambda j, i: (i, j)),
        out_shape=jax.ShapeDtypeStruct((m, ncols), out_dtype),
        scratch_shapes=[pltpu.VMEM((k, tn), BF16)],
        compiler_params=_params("parallel", "arbitrary"),
    )(x, w_all)


def _merge_kernel(y_ref, u_ref, w1_ref, w2_ref, g1_ref, g2_ref, o_ref):
    a = jnp.dot(y_ref[...], w1_ref[...], preferred_element_type=F32)
    b = jnp.dot(u_ref[...], w2_ref[...], preferred_element_type=F32)
    o_ref[...] = (_sigmoid(g1_ref[...]) * a + _sigmoid(g2_ref[...]) * b).astype(o_ref.dtype)


def _merge(yn, ucv, w1, w2, gg, gate_col0, *, tm, tn):
    m, k = yn.shape
    n = w1.shape[1]
    tm, tn = _tile(m, tm), _tile(n, tn)
    c1 = gate_col0 // tn
    c2 = (gate_col0 + n) // tn
    return pl.pallas_call(
        _merge_kernel,
        grid=(m // tm, n // tn),
        in_specs=[pl.BlockSpec((tm, k), lambda i, j: (i, 0)),
                  pl.BlockSpec((tm, k), lambda i, j: (i, 0)),
                  pl.BlockSpec((k, tn), lambda i, j: (0, j)),
                  pl.BlockSpec((k, tn), lambda i, j: (0, j)),
                  pl.BlockSpec((tm, tn), lambda i, j: (i, c1 + j)),
                  pl.BlockSpec((tm, tn), lambda i, j: (i, c2 + j))],
        out_specs=pl.BlockSpec((tm, tn), lambda i, j: (i, j)),
        out_shape=jax.ShapeDtypeStruct((m, n), BF16),
        compiler_params=_params("parallel", "parallel"),
    )(yn, ucv, w1, w2, gg, gg)


def _mod_kernel(c_ref, w_ref, b_ref, o_ref):
    cs = _silu(c_ref[...]).astype(BF16)
    o_ref[...] = jnp.dot(cs, w_ref[...].astype(BF16), preferred_element_type=F32) + b_ref[...]


def _modulation(cvec, w_mod, b_mod):
    depth, d, n = w_mod.shape
    tn = _tile(n, 512)
    return pl.pallas_call(
        _mod_kernel,
        grid=(depth, n // tn),
        in_specs=[pl.BlockSpec((MOD_ROWS, d), lambda l, j: (0, 0)),
                  pl.BlockSpec((None, d, tn), lambda l, j: (l, 0, j)),
                  pl.BlockSpec((None, 1, tn), lambda l, j: (l, 0, j))],
        out_specs=pl.BlockSpec((None, MOD_ROWS, tn), lambda l, j: (l, 0, j)),
        out_shape=jax.ShapeDtypeStruct((depth, MOD_ROWS, n), F32),
        compiler_params=_params("parallel", "parallel"),
    )(cvec, w_mod, b_mod.reshape(depth, 1, n))


def _prenorm_kernel(xc_ref, xl_ref, g_ref, sc_ref, sh_ref, xo_ref, h_ref, *, n_ctx_blk):
    x = jnp.where(pl.program_id(0) < n_ctx_blk, xc_ref[...], xl_ref[...])
    xo_ref[...] = x
    h_ref[...] = (_rms(x, g_ref[...]) * (1.0 + sc_ref[...]) + sh_ref[...]).astype(h_ref.dtype)


def _resnorm_kernel(x_ref, r_ref, ga_ref, gt_ref, gb_ref, sc_ref, sh_ref, xo_ref, h_ref):
    xn = x_ref[...] + gt_ref[...] * _rms(r_ref[...], ga_ref[...])
    xo_ref[...] = xn
    h_ref[...] = (_rms(xn, gb_ref[...]) * (1.0 + sc_ref[...]) + sh_ref[...]).astype(h_ref.dtype)


def _res_kernel(x_ref, r_ref, ga_ref, gt_ref, xo_ref):
    xo_ref[...] = x_ref[...] + gt_ref[...] * _rms(r_ref[...], ga_ref[...])


def _mod_spec(d, which, n_ctx_blk, lat_blk_per_req):
    def index(i):
        row = (i >= n_ctx_blk).astype(jnp.int32) * (1 + (i - n_ctx_blk) // lat_blk_per_req)
        return (row * 6 + which, 0, 0)
    return pl.BlockSpec((None, 1, d), index)


def _row_spec(d):
    return pl.BlockSpec((ROW_BLOCK, d), lambda i: (i, 0))


def _vec_spec(d):
    return pl.BlockSpec((1, d), lambda i: (0, 0))


def _prenorm(x_ctx, x_lat, g, mods, sc_idx, sh_idx, geo):
    d = x_ctx.shape[1]
    t = x_ctx.shape[0] + x_lat.shape[0]
    n_ctx_blk = geo[0]
    ms = functools.partial(_mod_spec, d, n_ctx_blk=n_ctx_blk, lat_blk_per_req=geo[1])
    return pl.pallas_call(
        functools.partial(_prenorm_kernel, n_ctx_blk=n_ctx_blk),
        grid=(t // ROW_BLOCK,),
        in_specs=[pl.BlockSpec((ROW_BLOCK, d), lambda i: (jnp.minimum(i, n_ctx_blk - 1), 0)),
                  pl.BlockSpec((ROW_BLOCK, d), lambda i: (jnp.maximum(i - n_ctx_blk, 0), 0)),
                  _vec_spec(d), ms(sc_idx), ms(sh_idx)],
        out_specs=[_row_spec(d), _row_spec(d)],
        out_shape=[jax.ShapeDtypeStruct((t, d), F32), jax.ShapeDtypeStruct((t, d), BF16)],
        compiler_params=_params("parallel"),
    )(x_ctx, x_lat, g.reshape(1, d), mods, mods)


def _resnorm(x, r, ga, mods_gate, gt_idx, gb, mods_next, sc_idx, sh_idx, geo):
    t, d = x.shape
    ms = functools.partial(_mod_spec, d, n_ctx_blk=geo[0], lat_blk_per_req=geo[1])
    return pl.pallas_call(
        _resnorm_kernel,
        grid=(t // ROW_BLOCK,),
        in_specs=[_row_spec(d), _row_spec(d), _vec_spec(d), ms(gt_idx), _vec_spec(d), ms(sc_idx), ms(sh_idx)],
        out_specs=[_row_spec(d), _row_spec(d)],
        out_shape=[jax.ShapeDtypeStruct((t, d), F32), jax.ShapeDtypeStruct((t, d), BF16)],
        compiler_params=_params("parallel"),
    )(x, r, ga.reshape(1, d), mods_gate, gb.reshape(1, d), mods_next, mods_next)


def _res(x, r, ga, mods_gate, gt_idx, geo):
    t, d = x.shape
    ms = functools.partial(_mod_spec, d, n_ctx_blk=geo[0], lat_blk_per_req=geo[1])
    return pl.pallas_call(
        _res_kernel,
        grid=(t // ROW_BLOCK,),
        in_specs=[_row_spec(d), _row_spec(d), _vec_spec(d), ms(gt_idx)],
        out_specs=_row_spec(d),
        out_shape=jax.ShapeDtypeStruct((t, d), F32),
        compiler_params=_params("parallel"),
    )(x, r, ga.reshape(1, d), mods_gate)


def _gated_norm_kernel(y_ref, z_ref, g_ref, o_ref):
    o_ref[...] = _rms(y_ref[...] * _silu(z_ref[...]), g_ref[...]).astype(o_ref.dtype)


def _gated_norm(y, z, g):
    t, d = y.shape
    return pl.pallas_call(
        _gated_norm_kernel,
        grid=(t // ROW_BLOCK,),
        in_specs=[_row_spec(d), _row_spec(d), _vec_spec(d)],
        out_specs=_row_spec(d),
        out_shape=jax.ShapeDtypeStruct((t, d), BF16),
        compiler_params=_params("parallel"),
    )(y, z, g.reshape(1, d))


def _ln_silu_kernel(u_ref, g_ref, b_ref, o_ref):
    u = u_ref[...]
    uc = u - jnp.mean(u, axis=-1, keepdims=True)
    var = jnp.mean(uc * uc, axis=-1, keepdims=True)
    o_ref[...] = _silu(uc * lax.rsqrt(var + EPS) * g_ref[...] + b_ref[...]).astype(o_ref.dtype)


def _ln_silu(u, g, b):
    t, d = u.shape
    return pl.pallas_call(
        _ln_silu_kernel,
        grid=(t // ROW_BLOCK,),
        in_specs=[_row_spec(d), _vec_spec(d), _vec_spec(d)],
        out_specs=_row_spec(d),
        out_shape=jax.ShapeDtypeStruct((t, d), BF16),
        compiler_params=_params("parallel"),
    )(u, g.reshape(1, d), b.reshape(1, d))


def _seg_len(n_ctx_blk, seq):
    return jnp.where(pl.program_id(0) < n_ctx_blk, seq, GRID_W)


def _conv3_masks(r0, seg):
    rows = lax.broadcasted_iota(jnp.int32, (8, 128), 0) + r0
    return (rows & (seg - 1)) != 0, ((rows + CONV_PIECE - 8) & (seg - 1)) != seg - 1


def _conv3_unit(x_ref, w_ref, b_ref, r0, c0, keep_prev, keep_next):
    n = CONV_PIECE
    cs = slice(c0, c0 + 128)
    x = x_ref[r0:r0 + n, cs]
    prev = x_ref[r0 - 1:r0 + n - 1, cs] if r0 > 0 else pltpu.roll(x, 1, 0)
    nxt = x_ref[r0 + 1:r0 + n + 1, cs] if r0 + n < x_ref.shape[0] else pltpu.roll(x, n - 1, 0)
    prev = jnp.concatenate([jnp.where(keep_prev, prev[:8], 0.0), prev[8:]], axis=0)
    nxt = jnp.concatenate([nxt[:n - 8], jnp.where(keep_next, nxt[n - 8:], 0.0)], axis=0)
    acc = w_ref[1:2, cs] * x + b_ref[:, cs]
    acc = acc + w_ref[0:1, cs] * prev
    return acc + w_ref[2:3, cs] * nxt


def _ffn_act_kernel(ug_ref, uv_ref, wg_ref, wv_ref, bg_ref, bv_ref, o_ref, *, n_ctx_blk, seq):
    seg = _seg_len(n_ctx_blk, seq)
    rows, cols = o_ref.shape
    for r0 in range(0, rows, CONV_PIECE):
        kp, kn = _conv3_masks(r0, seg)
        for c0 in range(0, cols, 128):
            gate = _conv3_unit(ug_ref, wg_ref, bg_ref, r0, c0, kp, kn)
            val = _conv3_unit(uv_ref, wv_ref, bv_ref, r0, c0, kp, kn)
            o_ref[r0:r0 + CONV_PIECE, c0:c0 + 128] = (_silu(gate) * val).astype(o_ref.dtype)


def _ffn_act(u, w, b, n_ctx_rows, seq, *, tr, tc):
    t, f2 = u.shape
    f = f2 // 2
    tr, tc = _tile(n_ctx_rows, tr), _tile(f, tc)
    nj = f // tc
    kw = w.shape[0]
    b2 = b.reshape(1, f2)
    return pl.pallas_call(
        functools.partial(_ffn_act_kernel, n_ctx_blk=n_ctx_rows // tr, seq=seq),
        grid=(t // tr, nj),
        in_specs=[pl.BlockSpec((tr, tc), lambda i, j: (i, j)),
                  pl.BlockSpec((tr, tc), lambda i, j: (i, nj + j)),
                  pl.BlockSpec((kw, tc), lambda i, j: (0, j)),
                  pl.BlockSpec((kw, tc), lambda i, j: (0, nj + j)),
                  pl.BlockSpec((1, tc), lambda i, j: (0, j)),
                  pl.BlockSpec((1, tc), lambda i, j: (0, nj + j))],
        out_specs=pl.BlockSpec((tr, tc), lambda i, j: (i, j)),
        out_shape=jax.ShapeDtypeStruct((t, f), BF16),
        compiler_params=_params("parallel", "parallel"),
    )(u, u, w, w, b2, b2)


def _glu_conv_kernel(ua_ref, ug_ref, w_ref, b_ref, o_ref, pad_ref, *, n_ctx_blk, seq, kw):
    rows, cols = o_ref.shape
    half = kw // 2
    piece = min(GRID_W, rows)

    def run(seg):
        stride = seg + 2 * CONV_PAD
        used = (rows // seg) * stride
        zeros = jnp.zeros((CONV_PAD, cols), F32)
        for s in range(rows // seg):
            base = s * stride
            src = slice(s * seg, (s + 1) * seg)
            pad_ref[0, base:base + CONV_PAD, :] = zeros
            pad_ref[0, base + CONV_PAD:base + CONV_PAD + seg, :] = ua_ref[src, :] * _sigmoid(ug_ref[src, :])
            pad_ref[0, base + CONV_PAD + seg:base + stride, :] = zeros
        for r in range(1, 8):
            pad_ref[r, 0:used - 8, :] = pad_ref[0, r:used - 8 + r, :]
        for s in range(rows // seg):
            for p in range(seg // piece):
                a = s * stride + p * piece
                acc = jnp.zeros((piece, cols), F32)
                for k in range(kw):
                    off = CONV_PAD - half + k
                    acc = acc + w_ref[k:k + 1, :] * pad_ref[off % 8, a + off - off % 8:a + off - off % 8 + piece, :]
                out0 = s * seg + p * piece
                o_ref[out0:out0 + piece, :] = acc + b_ref[...]

    is_ctx = pl.program_id(0) < n_ctx_blk

    @pl.when(is_ctx)
    def _():
        run(min(seq, rows))

    @pl.when(jnp.logical_not(is_ctx))
    def _():
        run(GRID_W)


def _glu_conv(gg, w, b, n_ctx_rows, seq, *, tc):
    t = gg.shape[0]
    kw, c = w.shape
    assert kw // 2 <= CONV_PAD
    tr = ROW_BLOCK
    tc = _tile(c, tc)
    nj = c // tc
    pad_rows = (tr // GRID_W) * (GRID_W + 2 * CONV_PAD)
    return pl.pallas_call(
        functools.partial(_glu_conv_kernel, n_ctx_blk=n_ctx_rows // tr, seq=seq, kw=kw),
        grid=(t // tr, nj),
        in_specs=[pl.BlockSpec((tr, tc), lambda i, j: (i, j)),
                  pl.BlockSpec((tr, tc), lambda i, j: (i, nj + j)),
                  pl.BlockSpec((kw, tc), lambda i, j: (0, j)),
                  pl.BlockSpec((1, tc), lambda i, j: (0, j))],
        out_specs=pl.BlockSpec((tr, tc), lambda i, j: (i, j)),
        out_shape=jax.ShapeDtypeStruct((t, c), F32),
        scratch_shapes=[pltpu.VMEM((8, pad_rows, tc), F32)],
        compiler_params=_params("parallel", "parallel"),
    )(gg, gg, w, b.reshape(1, c))


def _spread_selector(which, heads, reps, rows):
    assert reps & (reps - 1) == 0
    out_lanes = heads * reps
    k = lax.broadcasted_iota(jnp.int32, (rows, out_lanes), 0)
    head = lax.shift_right_logical(lax.broadcasted_iota(jnp.int32, (rows, out_lanes), 1), reps.bit_length() - 1)
    hit = k == head + which * heads
    for piece in (1, 2):
        hit = jnp.logical_or(hit, k == head + (piece * 3 + which) * heads)
    return hit.astype(BF16)


def _ssd_kernel(x_ref, b_ref, c_ref, wx_ref, wb_ref, wc_ref, cbx_ref, cbb_ref, cbc_ref,
                dtr_ref, br_ref, ar_ref, dsk_ref, h0_ref,
                y_ref, st_ref, xs_scr, btmp_scr, bmt_scr, cm_scr, ht_scr, sela_scr, selb_scr, ec_scr, st_scr,
                *, n_ctx_blk, heads, hdim, seq, seq_chunks):
    q = CHUNK
    n = CONV_PIECE
    block_rows, rp = x_ref.shape
    n_chunks = block_rows // q
    n_seqs = n_chunks // seq_chunks
    is_ctx = pl.program_id(0) < n_ctx_blk
    seg = _seg_len(n_ctx_blk, seq)
    for r0 in range(0, block_rows, n):
        kp, kn = _conv3_masks(r0, seg)
        for c0 in range(0, rp, 128):
            xa = _silu(_conv3_unit(x_ref, wx_ref, cbx_ref, r0, c0, kp, kn))
            xs_scr[r0:r0 + n, c0:c0 + 128] = xa
            y_ref[r0:r0 + n, c0:c0 + 128] = xa * dsk_ref[:, c0:c0 + 128]
        btmp_scr[r0:r0 + n, :] = _silu(_conv3_unit(b_ref, wb_ref, cbb_ref, r0, 0, kp, kn))
        cm_scr[r0:r0 + n, :] = _silu(_conv3_unit(c_ref, wc_ref, cbc_ref, r0, 0, kp, kn)).astype(BF16)
    for c in range(n_chunks):
        bmt_scr[c] = btmp_scr[c * q:(c + 1) * q, :].T.astype(BF16)
    for d in range(2):
        ht_scr[d] = jnp.where(is_ctx, 0.0, h0_ref[d].T)
    sela_scr[...] = _spread_selector(0, heads, q, q)
    selb_scr[...] = jnp.concatenate([_spread_selector(1, heads, hdim, q), _spread_selector(2, heads, hdim, q)], axis=1)

    row = lax.broadcasted_iota(jnp.int32, (q, q), 0)
    col = lax.broadcasted_iota(jnp.int32, (q, q), 1)
    masks = (row >= col, row <= col)
    tris = (masks[0].astype(F32), masks[1].astype(F32))
    lane = lax.broadcasted_iota(jnp.int32, (q, 128), 1)
    a_r = -jnp.exp(ar_ref[...])
    per = 128 // hdim
    pad = jnp.zeros((q - 9 * heads, q), F32)

    def chunk_local(ci, d):
        rows = slice(ci * q, (ci + 1) * q)
        hs = slice(d * heads, (d + 1) * heads)
        dt = _softplus(dtr_ref[ci] + br_ref[...])[hs, :]
        cum = jnp.dot(dt * a_r[hs, :], tris[1 - d], precision=lax.Precision.HIGHEST, preferred_element_type=F32)
        bmt = bmt_scr[ci]
        scores = jnp.dot(cm_scr[rows, :], bmt, preferred_element_type=F32)
        yield
        end = q - 1 if d == 0 else 0
        cum_j = cum - jnp.log(dt)
        v = jnp.concatenate([cum, jnp.exp(cum), dt * jnp.exp(cum[:, end:end + 1] - cum)], axis=0)
        hi = v.astype(BF16).astype(F32)
        mid = (v - hi).astype(BF16).astype(F32)
        lo = (v - hi - mid).astype(BF16).astype(F32)
        vt = jnp.concatenate([hi, mid, lo, pad], axis=0).T.astype(BF16)
        yield
        cum_i = jnp.dot(vt, sela_scr[...], preferred_element_type=F32)
        spread = jnp.dot(vt, selb_scr[...], preferred_element_type=F32)
        yield
        x = xs_scr[rows, :]
        ec_scr[d, ci] = spread[:, :rp]
        st_scr[d, ci] = jnp.dot(bmt, (x * spread[:, rp:]).astype(BF16), preferred_element_type=F32)
        y_blocks = []
        for g in range(heads // per):
            xg = x[:, g * 128:(g + 1) * 128]
            w_parts, x_parts = [], []
            for k in range(per):
                r = g * per + k
                seg_ij = cum_i[:, r * q:(r + 1) * q] - cum_j[r:r + 1, :]
                w = scores * jnp.exp(jnp.where(masks[d], seg_ij, -jnp.inf))
                w_parts.append(w.astype(BF16))
                own = jnp.logical_and(lane >= k * hdim, lane < (k + 1) * hdim)
                x_parts.append(jnp.where(own, xg, 0.0).astype(BF16))
            y_blocks.append(jnp.dot(jnp.concatenate(w_parts, axis=1), jnp.concatenate(x_parts, axis=0),
                                    preferred_element_type=F32))
            yield
        y_ref[rows, :] += jnp.concatenate(y_blocks, axis=1)

    for t in range(n_chunks // LOCKSTEP_CHUNKS):
        chains = [chunk_local(t * LOCKSTEP_CHUNKS + k, d) for k in range(LOCKSTEP_CHUNKS) for d in range(2)]
        for _ in itertools.zip_longest(*chains):
            pass

    def carry_state(ci, d, restart):
        rows = slice(ci * q, (ci + 1) * q)
        end = q - 1 if d == 0 else 0
        h_in = ht_scr[d]
        if restart:
            h_in = jnp.where(is_ctx, 0.0, h_in)
        e_cum = ec_scr[d, ci]
        y_ref[rows, :] += jnp.dot(cm_scr[rows, :], h_in.astype(BF16), preferred_element_type=F32) * e_cum
        ht_scr[d] = e_cum[end:end + 1, :] * h_in + st_scr[d, ci]

    for s in range(n_seqs):
        for k in range(seq_chunks):
            ci = s * seq_chunks + k
            carry_state(ci, 0, k == 0)
            carry_state(n_chunks - 1 - ci, 1, k == 0)
        st_ref[s, 0] = ht_scr[0].T
        st_ref[n_seqs - 1 - s, 1] = ht_scr[1].T


def _ssd(xd, conv_w, conv_b, dtr, dt_bias, a_log, d_skip, state_ssm, layer, *, d_ssm, groups, n_state,
         n_ctx_rows, seq, block_rows):
    t = xd.shape[0]
    two_r = dtr.shape[-2]
    heads = two_r // 2
    rp = d_ssm // groups
    hdim = rp // heads
    assert 128 % hdim == 0 and heads % (128 // hdim) == 0 and rp % 128 == 0 and 9 * heads <= CHUNK
    assert block_rows % (LOCKSTEP_CHUNKS * CHUNK) == 0
    n_ctx_blk = n_ctx_rows // block_rows
    seqs = block_rows // seq
    n_chunks = block_rows // CHUNK
    nb = d_ssm // n_state
    kw, d_xbc = conv_w.shape
    dec_b, depth = state_ssm.shape[:2]
    h0 = state_ssm.reshape(dec_b, depth, 2, d_ssm, n_state)
    bias = dt_bias.reshape(2, groups, heads).transpose(1, 0, 2).reshape(groups, two_r, 1)
    alog = a_log.reshape(2, groups, heads).transpose(1, 0, 2).reshape(groups, two_r, 1)
    dsk = jnp.repeat(d_skip, hdim).reshape(groups, 1, rp)
    cb = conv_b.reshape(1, d_xbc)
    kern = functools.partial(_ssd_kernel, n_ctx_blk=n_ctx_blk, heads=heads, hdim=hdim, seq=seq,
                             seq_chunks=seq // CHUNK)
    return pl.pallas_call(
        kern,
        grid=(t // block_rows, groups),
        in_specs=[pl.BlockSpec((block_rows, rp), lambda b, g: (b, g)),
                  pl.BlockSpec((block_rows, n_state), lambda b, g: (b, nb + g)),
                  pl.BlockSpec((block_rows, n_state), lambda b, g: (b, nb + groups + g)),
                  pl.BlockSpec((kw, rp), lambda b, g: (0, g)),
                  pl.BlockSpec((kw, n_state), lambda b, g: (0, nb + g)),
                  pl.BlockSpec((kw, n_state), lambda b, g: (0, nb + groups + g)),
                  pl.BlockSpec((1, rp), lambda b, g: (0, g)),
                  pl.BlockSpec((1, n_state), lambda b, g: (0, nb + g)),
                  pl.BlockSpec((1, n_state), lambda b, g: (0, nb + groups + g)),
                  pl.BlockSpec((None, n_chunks, two_r, CHUNK), lambda b, g: (g, b, 0, 0)),
                  pl.BlockSpec((None, two_r, 1), lambda b, g: (g, 0, 0)),
                  pl.BlockSpec((None, two_r, 1), lambda b, g: (g, 0, 0)),
                  pl.BlockSpec((None, 1, rp), lambda b, g: (g, 0, 0)),
                  pl.BlockSpec((None, None, 2, rp, n_state),
                               lambda b, g: (jnp.maximum(b - n_ctx_blk, 0), layer, 0, g, 0))],
        out_specs=[pl.BlockSpec((block_rows, rp), lambda b, g: (b, g)),
                   pl.BlockSpec((seqs, 2, rp, n_state), lambda b, g: (b, 0, g, 0))],
        out_shape=[jax.ShapeDtypeStruct((t, d_ssm), F32),
                   jax.ShapeDtypeStruct((t // seq, 2, d_ssm, n_state), F32)],
        scratch_shapes=[pltpu.VMEM((block_rows, rp), F32), pltpu.VMEM((block_rows, n_state), F32),
                        pltpu.VMEM((n_chunks, n_state, CHUNK), BF16), pltpu.VMEM((block_rows, n_state), BF16),
                        pltpu.VMEM((2, n_state, rp), F32),
                        pltpu.VMEM((CHUNK, heads * CHUNK), BF16), pltpu.VMEM((CHUNK, 2 * rp), BF16),
                        pltpu.VMEM((2, n_chunks, CHUNK, rp), F32), pltpu.VMEM((2, n_chunks, n_state, rp), F32)],
        compiler_params=_params("parallel", "parallel"),
    )(xd, xd, xd, conv_w, conv_w, conv_w, cb, cb, cb, dtr, bias, alog, dsk, h0)


def kernel(x_prompt, x_sample, state_ssm, c, c_ctx, w_mod, b_mod, g_pre_mix, g_post_mix, g_pre_ffn, g_post_ffn, w_in, ssm_conv_w, ssm_conv_b, dt_bias, a_log, d_skip, g_ssm_norm, w_br_ssm, cv_conv_w, cv_conv_b, cv_ln_g, cv_ln_b, w_cv_out, w_out, w_up, ffn_conv_w, ffn_conv_b, w_down):
    batch, seq, d = x_prompt.shape
    dec_b, dec_seq, _ = x_sample.shape
    depth = w_mod.shape[0]
    n_state = state_ssm.shape[-1]
    heads_all = dt_bias.shape[-1]
    d_ssm = g_ssm_norm.shape[-1]
    d_xbc = ssm_conv_w.shape[-1]
    d_conv = cv_conv_w.shape[-1]
    groups = (d_xbc - d_ssm) // (2 * n_state)
    heads = heads_all // groups
    n_ctx_rows = batch * seq
    t = n_ctx_rows + dec_b * dec_seq
    assert seq % ROW_BLOCK == 0 and dec_seq % seq == 0 and dec_seq % GRID_W == 0 and ROW_BLOCK % GRID_W == 0
    assert n_ctx_rows % dec_seq == 0 and 1 + dec_b <= MOD_ROWS
    assert seq & (seq - 1) == 0 and GRID_W & (GRID_W - 1) == 0 and GRID_W % CONV_PIECE == 0
    geo = (n_ctx_rows // ROW_BLOCK, dec_seq // ROW_BLOCK)
    z_end = d_ssm
    dt_end = z_end + d_xbc + 2 * heads_all
    in_cols = w_in.shape[-1]

    cvec = jnp.concatenate([c_ctx[None, :], c, jnp.zeros((MOD_ROWS - 1 - dec_b, d), F32)], axis=0)
    mods = _modulation(cvec, w_mod, b_mod).reshape(depth, MOD_ROWS * 6, 1, d)

    x, h = _prenorm(x_prompt.reshape(n_ctx_rows, d), x_sample.reshape(dec_b * dec_seq, d), g_pre_mix[0], mods[0],
                    1, 0, geo)
    states = []
    for l in range(depth):
        z = _matmul_wcast(h, w_in, l, 0, z_end, tm=512, tn=1024)
        xd = _matmul_wcast(h, w_in, l, z_end, d_xbc, tm=512, tn=1024)
        dtm = _matmul_wcast(h, w_in, l, z_end + d_xbc, 2 * heads_all, tm=1024, tn=128)
        gg = _matmul_wcast(h, w_in, l, dt_end, in_cols - dt_end, tm=512, tn=1024)

        dt_raw = dtm.reshape(t, 2, groups, heads)
        dtr = dt_raw.transpose(2, 1, 3, 0).reshape(groups, 2 * heads, t // CHUNK, CHUNK).transpose(0, 2, 1, 3)
        y, st = _ssd(xd, ssm_conv_w[l], ssm_conv_b[l], dtr, dt_bias[l], a_log[l], d_skip[l], state_ssm, l,
                     d_ssm=d_ssm, groups=groups, n_state=n_state, n_ctx_rows=n_ctx_rows, seq=seq,
                     block_rows=dec_seq)
        states.append(st[:batch])
        yn = _gated_norm(y, z, g_ssm_norm[l])

        u = _glu_conv(gg, cv_conv_w[l], cv_conv_b[l], n_ctx_rows, seq, tc=512)
        ucv = _ln_silu(u, cv_ln_g[l], cv_ln_b[l])

        merged = _merge(yn, ucv, w_br_ssm[l].astype(BF16), w_cv_out[l].astype(BF16), gg, 2 * d_conv,
                        tm=512, tn=512)
        m = _matmul_wcast(merged, w_out, l, 0, d, tm=512, tn=1024)
        x, h = _resnorm(x, m, g_post_mix[l], mods[l], 2, g_pre_ffn[l], mods[l], 4, 3, geo)

        up = _matmul_wcast(h, w_up, l, 0, w_up.shape[-1], tm=1024, tn=512)
        act = _ffn_act(up, ffn_conv_w[l], ffn_conv_b[l], n_ctx_rows, seq, tr=256, tc=5504)
        f = _matmul(act, w_down[l].astype(BF16), tm=512, tn=512)
        if l + 1 < depth:
            x, h = _resnorm(x, f, g_post_ffn[l], mods[l], 5, g_pre_mix[l + 1], mods[l + 1], 1, 0, geo)
        else:
            x = _res(x, f, g_post_ffn[l], mods[l], 5, geo)

    y_prompt = x[:n_ctx_rows].reshape(batch, seq, d)
    y_sample = x[n_ctx_rows:].reshape(dec_b, dec_seq, d)
    new_state = jnp.stack(states, axis=1).reshape(batch, depth, 2, heads_all, d_ssm // heads_all, n_state)
    return (y_prompt, y_sample, new_state)
```

```python
import functools
import itertools

import jax
import jax.numpy as jnp
from jax import lax
from jax.experimental import pallas as pl
from jax.experimental.pallas import tpu as pltpu

GRID_W = 64
CHUNK = 128
EPS = 1e-6
ROW_BLOCK = 256
CONV_PAD = 16
CONV_PIECE = 64
LOCKSTEP_CHUNKS = 2
MOD_ROWS = 16
V7X_VMEM_LIMIT = 56 * 1024 * 1024

F32 = jnp.float32
BF16 = jnp.bfloat16


def _params(*sem):
    return pltpu.CompilerParams(dimension_semantics=sem, vmem_limit_bytes=V7X_VMEM_LIMIT)


def _tile(n, pref):
    if n <= pref:
        return n
    t = (pref // 128) * 128
    while n % t:
        t -= 128
    return t


def _sigmoid(x):
    return 1.0 / (1.0 + jnp.exp(-x))


def _silu(x):
    return x * _sigmoid(x)


def _softplus(x):
    return jnp.maximum(x, 0.0) + jnp.log1p(jnp.exp(-jnp.abs(x)))


def _rms(x, g):
    return x * lax.rsqrt(jnp.mean(x * x, axis=-1, keepdims=True) + EPS) * g


def _mm_kernel(x_ref, w_ref, o_ref):
    o_ref[...] = jnp.dot(x_ref[...], w_ref[...], preferred_element_type=F32).astype(o_ref.dtype)


def _matmul(x, w, *, tm, tn, out_dtype=F32):
    m, k = x.shape
    n = w.shape[1]
    tm, tn = _tile(m, tm), _tile(n, tn)
    return pl.pallas_call(
        _mm_kernel,
        grid=(m // tm, n // tn),
        in_specs=[pl.BlockSpec((tm, k), lambda i, j: (i, 0)),
                  pl.BlockSpec((k, tn), lambda i, j: (0, j))],
        out_specs=pl.BlockSpec((tm, tn), lambda i, j: (i, j)),
        out_shape=jax.ShapeDtypeStruct((m, n), out_dtype),
        compiler_params=_params("parallel", "parallel"),
    )(x, w)


def _mm_wcast_kernel(x_ref, w_ref, o_ref, wbf_scr):
    @pl.when(pl.program_id(1) == 0)
    def _():
        wbf_scr[...] = w_ref[0].astype(BF16)

    o_ref[...] = jnp.dot(x_ref[...], wbf_scr[...], preferred_element_type=F32).astype(o_ref.dtype)


def _matmul_wcast(x, w_all, layer, col0, ncols, *, tm, tn, out_dtype=F32):
    m, k = x.shape
    if col0 % 128 or ncols % 128:
        return _matmul(x, w_all[layer, :, col0:col0 + ncols].astype(BF16), tm=tm, tn=tn, out_dtype=out_dtype)
    tm, tn = _tile(m, tm), _tile(ncols, tn)
    return pl.pallas_call(
        _mm_wcast_kernel,
        grid=(ncols // tn, m // tm),
        in_specs=[pl.BlockSpec((tm, k), lambda j, i: (i, 0)),
                  pl.BlockSpec((pl.Element(1), pl.Element(k), pl.Element(tn)),
                               lambda j, i: (layer, 0, pl.multiple_of(col0 + j * tn, 128)))],
        out_specs=pl.BlockSpec((tm, tn), lambda j, i: (i, j)),
        out_shape=jax.ShapeDtypeStruct((m, ncols), out_dtype),
        scratch_shapes=[pltpu.VMEM((k, tn), BF16)],
        compiler_params=_params("parallel", "arbitrary"),
    )(x, w_all)


def _merge_kernel(y_ref, u_ref, w1_ref, w2_ref, g1_ref, g2_ref, o_ref):
    a = jnp.dot(y_ref[...], w1_ref[...], preferred_element_type=F32)
    b = jnp.dot(u_ref[...], w2_ref[...], preferred_element_type=F32)
    o_ref[...] = (_sigmoid(g1_ref[...]) * a + _sigmoid(g2_ref[...]) * b).astype(o_ref.dtype)


def _merge(yn, ucv, w1, w2, gg, gate_col0, *, tm, tn):
    m, k = yn.shape
    n = w1.shape[1]
    tm, tn = _tile(m, tm), _tile(n, tn)
    c1 = gate_col0 // tn
    c2 = (gate_col0 + n) // tn
    return pl.pallas_call(
        _merge_kernel,
        grid=(m // tm, n // tn),
        in_specs=[pl.BlockSpec((tm, k), lambda i, j: (i, 0)),
                  pl.BlockSpec((tm, k), lambda i, j: (i, 0)),
                  pl.BlockSpec((k, tn), lambda i, j: (0, j)),
                  pl.BlockSpec((k, tn), lambda i, j: (0, j)),
                  pl.BlockSpec((tm, tn), lambda i, j: (i, c1 + j)),
                  pl.BlockSpec((tm, tn), lambda i, j: (i, c2 + j))],
        out_specs=pl.BlockSpec((tm, tn), lambda i, j: (i, j)),
        out_shape=jax.ShapeDtypeStruct((m, n), BF16),
        compiler_params=_params("parallel", "parallel"),
    )(yn, ucv, w1, w2, gg, gg)


def _mod_kernel(c_ref, w_ref, b_ref, o_ref):
    cs = _silu(c_ref[...]).astype(BF16)
    o_ref[...] = jnp.dot(cs, w_ref[...].astype(BF16), preferred_element_type=F32) + b_ref[...]


def _modulation(cvec, w_mod, b_mod):
    depth, d, n = w_mod.shape
    tn = _tile(n, 512)
    return pl.pallas_call(
        _mod_kernel,
        grid=(depth, n // tn),
        in_specs=[pl.BlockSpec((MOD_ROWS, d), lambda l, j: (0, 0)),
                  pl.BlockSpec((None, d, tn), lambda l, j: (l, 0, j)),
                  pl.BlockSpec((None, 1, tn), lambda l, j: (l, 0, j))],
        out_specs=pl.BlockSpec((None, MOD_ROWS, tn), lambda l, j: (l, 0, j)),
        out_shape=jax.ShapeDtypeStruct((depth, MOD_ROWS, n), F32),
        compiler_params=_params("parallel", "parallel"),
    )(cvec, w_mod, b_mod.reshape(depth, 1, n))


def _prenorm_kernel(xc_ref, xl_ref, g_ref, sc_ref, sh_ref, xo_ref, h_ref, *, n_ctx_blk):
    x = jnp.where(pl.program_id(0) < n_ctx_blk, xc_ref[...], xl_ref[...])
    xo_ref[...] = x
    h_ref[...] = (_rms(x, g_ref[...]) * (1.0 + sc_ref[...]) + sh_ref[...]).astype(h_ref.dtype)


def _resnorm_kernel(x_ref, r_ref, ga_ref, gt_ref, gb_ref, sc_ref, sh_ref, xo_ref, h_ref):
    xn = x_ref[...] + gt_ref[...] * _rms(r_ref[...], ga_ref[...])
    xo_ref[...] = xn
    h_ref[...] = (_rms(xn, gb_ref[...]) * (1.0 + sc_ref[...]) + sh_ref[...]).astype(h_ref.dtype)


def _res_kernel(x_ref, r_ref, ga_ref, gt_ref, xo_ref):
    xo_ref[...] = x_ref[...] + gt_ref[...] * _rms(r_ref[...], ga_ref[...])


def _mod_spec(d, which, n_ctx_blk, lat_blk_per_req):
    def index(i):
        row = (i >= n_ctx_blk).astype(jnp.int32) * (1 + (i - n_ctx_blk) // lat_blk_per_req)
        return (row * 6 + which, 0, 0)
    return pl.BlockSpec((None, 1, d), index)


def _row_spec(d):
    return pl.BlockSpec((ROW_BLOCK, d), lambda i: (i, 0))


def _vec_spec(d):
    return pl.BlockSpec((1, d), lambda i: (0, 0))


def _prenorm(x_ctx, x_lat, g, mods, sc_idx, sh_idx, geo):
    d = x_ctx.shape[1]
    t = x_ctx.shape[0] + x_lat.shape[0]
    n_ctx_blk = geo[0]
    ms = functools.partial(_mod_spec, d, n_ctx_blk=n_ctx_blk, lat_blk_per_req=geo[1])
    return pl.pallas_call(
        functools.partial(_prenorm_kernel, n_ctx_blk=n_ctx_blk),
        grid=(t // ROW_BLOCK,),
        in_specs=[pl.BlockSpec((ROW_BLOCK, d), lambda i: (jnp.minimum(i, n_ctx_blk - 1), 0)),
                  pl.BlockSpec((ROW_BLOCK, d), lambda i: (jnp.maximum(i - n_ctx_blk, 0), 0)),
                  _vec_spec(d), ms(sc_idx), ms(sh_idx)],
        out_specs=[_row_spec(d), _row_spec(d)],
        out_shape=[jax.ShapeDtypeStruct((t, d), F32), jax.ShapeDtypeStruct((t, d), BF16)],
        compiler_params=_params("parallel"),
    )(x_ctx, x_lat, g.reshape(1, d), mods, mods)


def _resnorm(x, r, ga, mods_gate, gt_idx, gb, mods_next, sc_idx, sh_idx, geo):
    t, d = x.shape
    ms = functools.partial(_mod_spec, d, n_ctx_blk=geo[0], lat_blk_per_req=geo[1])
    return pl.pallas_call(
        _resnorm_kernel,
        grid=(t // ROW_BLOCK,),
        in_specs=[_row_spec(d), _row_spec(d), _vec_spec(d), ms(gt_idx), _vec_spec(d), ms(sc_idx), ms(sh_idx)],
        out_specs=[_row_spec(d), _row_spec(d)],
        out_shape=[jax.ShapeDtypeStruct((t, d), F32), jax.ShapeDtypeStruct((t, d), BF16)],
        compiler_params=_params("parallel"),
    )(x, r, ga.reshape(1, d), mods_gate, gb.reshape(1, d), mods_next, mods_next)


def _res(x, r, ga, mods_gate, gt_idx, geo):
    t, d = x.shape
    ms = functools.partial(_mod_spec, d, n_ctx_blk=geo[0], lat_blk_per_req=geo[1])
    return pl.pallas_call(
        _res_kernel,
        grid=(t // ROW_BLOCK,),
        in_specs=[_row_spec(d), _row_spec(d), _vec_spec(d), ms(gt_idx)],
        out_specs=_row_spec(d),
        out_shape=jax.ShapeDtypeStruct((t, d), F32),
        compiler_params=_params("parallel"),
    )(x, r, ga.reshape(1, d), mods_gate)


def _gated_norm_kernel(y_ref, z_ref, g_ref, o_ref):
    o_ref[...] = _rms(y_ref[...] * _silu(z_ref[...]), g_ref[...]).astype(o_ref.dtype)


def _gated_norm(y, z, g):
    t, d = y.shape
    return pl.pallas_call(
        _gated_norm_kernel,
        grid=(t // ROW_BLOCK,),
        in_specs=[_row_spec(d), _row_spec(d), _vec_spec(d)],
        out_specs=_row_spec(d),
        out_shape=jax.ShapeDtypeStruct((t, d), BF16),
        compiler_params=_params("parallel"),
    )(y, z, g.reshape(1, d))


def _ln_silu_kernel(u_ref, g_ref, b_ref, o_ref):
    u = u_ref[...]
    uc = u - jnp.mean(u, axis=-1, keepdims=True)
    var = jnp.mean(uc * uc, axis=-1, keepdims=True)
    o_ref[...] = _silu(uc * lax.rsqrt(var + EPS) * g_ref[...] + b_ref[...]).astype(o_ref.dtype)


def _ln_silu(u, g, b):
    t, d = u.shape
    return pl.pallas_call(
        _ln_silu_kernel,
        grid=(t // ROW_BLOCK,),
        in_specs=[_row_spec(d), _vec_spec(d), _vec_spec(d)],
        out_specs=_row_spec(d),
        out_shape=jax.ShapeDtypeStruct((t, d), BF16),
        compiler_params=_params("parallel"),
    )(u, g.reshape(1, d), b.reshape(1, d))


def _seg_len(n_ctx_blk, seq):
    return jnp.where(pl.program_id(0) < n_ctx_blk, seq, GRID_W)


def _conv3_masks(r0, seg):
    rows = lax.broadcasted_iota(jnp.int32, (8, 128), 0) + r0
    return (rows & (seg - 1)) != 0, ((rows + CONV_PIECE - 8) & (seg - 1)) != seg - 1


def _conv3_unit(x_ref, w_ref, b_ref, r0, c0, keep_prev, keep_next):
    n = CONV_PIECE
    cs = slice(c0, c0 + 128)
    x = x_ref[r0:r0 + n, cs]
    prev = x_ref[r0 - 1:r0 + n - 1, cs] if r0 > 0 else pltpu.roll(x, 1, 0)
    nxt = x_ref[r0 + 1:r0 + n + 1, cs] if r0 + n < x_ref.shape[0] else pltpu.roll(x, n - 1, 0)
    prev = jnp.concatenate([jnp.where(keep_prev, prev[:8], 0.0), prev[8:]], axis=0)
    nxt = jnp.concatenate([nxt[:n - 8], jnp.where(keep_next, nxt[n - 8:], 0.0)], axis=0)
    acc = w_ref[1:2, cs] * x + b_ref[:, cs]
    acc = acc + w_ref[0:1, cs] * prev
    return acc + w_ref[2:3, cs] * nxt


def _ffn_act_kernel(ug_ref, uv_ref, wg_ref, wv_ref, bg_ref, bv_ref, o_ref, *, n_ctx_blk, seq):
    seg = _seg_len(n_ctx_blk, seq)
    rows, cols = o_ref.shape
    for r0 in range(0, rows, CONV_PIECE):
        kp, kn = _conv3_masks(r0, seg)
        for c0 in range(0, cols, 128):
            gate = _conv3_unit(ug_ref, wg_ref, bg_ref, r0, c0, kp, kn)
            val = _conv3_unit(uv_ref, wv_ref, bv_ref, r0, c0, kp, kn)
            o_ref[r0:r0 + CONV_PIECE, c0:c0 + 128] = (_silu(gate) * val).astype(o_ref.dtype)


def _ffn_act(u, w, b, n_ctx_rows, seq, *, tr, tc):
    t, f2 = u.shape
    f = f2 // 2
    tr, tc = _tile(n_ctx_rows, tr), _tile(f, tc)
    nj = f // tc
    kw = w.shape[0]
    b2 = b.reshape(1, f2)
    return pl.pallas_call(
        functools.partial(_ffn_act_kernel, n_ctx_blk=n_ctx_rows // tr, seq=seq),
        grid=(t // tr, nj),
        in_specs=[pl.BlockSpec((tr, tc), lambda i, j: (i, j)),
                  pl.BlockSpec((tr, tc), lambda i, j: (i, nj + j)),
                  pl.BlockSpec((kw, tc), lambda i, j: (0, j)),
                  pl.BlockSpec((kw, tc), lambda i, j: (0, nj + j)),
                  pl.BlockSpec((1, tc), lambda i, j: (0, j)),
                  pl.BlockSpec((1, tc), lambda i, j: (0, nj + j))],
        out_specs=pl.BlockSpec((tr, tc), lambda i, j: (i, j)),
        out_shape=jax.ShapeDtypeStruct((t, f), BF16),
        compiler_params=_params("parallel", "parallel"),
    )(u, u, w, w, b2, b2)


def _glu_conv_kernel(ua_ref, ug_ref, w_ref, b_ref, o_ref, pad_ref, *, n_ctx_blk, seq, kw):
    rows, cols = o_ref.shape
    half = kw // 2
    piece = min(GRID_W, rows)

    def run(seg):
        stride = seg + 2 * CONV_PAD
        used = (rows // seg) * stride
        zeros = jnp.zeros((CONV_PAD, cols), F32)
        for s in range(rows // seg):
            base = s * stride
            src = slice(s * seg, (s + 1) * seg)
            pad_ref[0, base:base + CONV_PAD, :] = zeros
            pad_ref[0, base + CONV_PAD:base + CONV_PAD + seg, :] = ua_ref[src, :] * _sigmoid(ug_ref[src, :])
            pad_ref[0, base + CONV_PAD + seg:base + stride, :] = zeros
        for r in range(1, 8):
            pad_ref[r, 0:used - 8, :] = pad_ref[0, r:used - 8 + r, :]
        for s in range(rows // seg):
            for p in range(seg // piece):
                a = s * stride + p * piece
                acc = jnp.zeros((piece, cols), F32)
                for k in range(kw):
                    off = CONV_PAD - half + k
                    acc = acc + w_ref[k:k + 1, :] * pad_ref[off % 8, a + off - off % 8:a + off - off % 8 + piece, :]
                out0 = s * seg + p * piece
                o_ref[out0:out0 + piece, :] = acc + b_ref[...]

    is_ctx = pl.program_id(0) < n_ctx_blk

    @pl.when(is_ctx)
    def _():
        run(min(seq, rows))

    @pl.when(jnp.logical_not(is_ctx))
    def _():
        run(GRID_W)


def _glu_conv(gg, w, b, n_ctx_rows, seq, *, tc):
    t = gg.shape[0]
    kw, c = w.shape
    assert kw // 2 <= CONV_PAD
    tr = ROW_BLOCK
    tc = _tile(c, tc)
    nj = c // tc
    pad_rows = (tr // GRID_W) * (GRID_W + 2 * CONV_PAD)
    return pl.pallas_call(
        functools.partial(_glu_conv_kernel, n_ctx_blk=n_ctx_rows // tr, seq=seq, kw=kw),
        grid=(t // tr, nj),
        in_specs=[pl.BlockSpec((tr, tc), lambda i, j: (i, j)),
                  pl.BlockSpec((tr, tc), lambda i, j: (i, nj + j)),
                  pl.BlockSpec((kw, tc), lambda i, j: (0, j)),
                  pl.BlockSpec((1, tc), lambda i, j: (0, j))],
        out_specs=pl.BlockSpec((tr, tc), lambda i, j: (i, j)),
        out_shape=jax.ShapeDtypeStruct((t, c), F32),
        scratch_shapes=[pltpu.VMEM((8, pad_rows, tc), F32)],
        compiler_params=_params("parallel", "parallel"),
    )(gg, gg, w, b.reshape(1, c))


def _spread_selector(which, heads, reps, rows):
    assert reps & (reps - 1) == 0
    out_lanes = heads * reps
    k = lax.broadcasted_iota(jnp.int32, (rows, out_lanes), 0)
    head = lax.shift_right_logical(lax.broadcasted_iota(jnp.int32, (rows, out_lanes), 1), reps.bit_length() - 1)
    hit = k == head + which * heads
    for piece in (1, 2):
        hit = jnp.logical_or(hit, k == head + (piece * 3 + which) * heads)
    return hit.astype(BF16)


def _ssd_kernel(x_ref, b_ref, c_ref, wx_ref, wb_ref, wc_ref, cbx_ref, cbb_ref, cbc_ref,
                dtr_ref, br_ref, ar_ref, dsk_ref, h0_ref,
                y_ref, st_ref, xs_scr, btmp_scr, bmt_scr, cm_scr, ht_scr, sela_scr, selb_scr, ec_scr, st_scr,
                *, n_ctx_blk, heads, hdim, seq, seq_chunks):
    q = CHUNK
    n = CONV_PIECE
    block_rows, rp = x_ref.shape
    n_chunks = block_rows // q
    n_seqs = n_chunks // seq_chunks
    is_ctx = pl.program_id(0) < n_ctx_blk
    seg = _seg_len(n_ctx_blk, seq)
    for r0 in range(0, block_rows, n):
        kp, kn = _conv3_masks(r0, seg)
        for c0 in range(0, rp, 128):
            xa = _silu(_conv3_unit(x_ref, wx_ref, cbx_ref, r0, c0, kp, kn))
            xs_scr[r0:r0 + n, c0:c0 + 128] = xa
            y_ref[r0:r0 + n, c0:c0 + 128] = xa * dsk_ref[:, c0:c0 + 128]
        btmp_scr[r0:r0 + n, :] = _silu(_conv3_unit(b_ref, wb_ref, cbb_ref, r0, 0, kp, kn))
        cm_scr[r0:r0 + n, :] = _silu(_conv3_unit(c_ref, wc_ref, cbc_ref, r0, 0, kp, kn)).astype(BF16)
    for c in range(n_chunks):
        bmt_scr[c] = btmp_scr[c * q:(c + 1) * q, :].T.astype(BF16)
    for d in range(2):
        ht_scr[d] = jnp.where(is_ctx, 0.0, h0_ref[d].T)
    sela_scr[...] = _spread_selector(0, heads, q, q)
    selb_scr[...] = jnp.concatenate([_spread_selector(1, heads, hdim, q), _spread_selector(2, heads, hdim, q)], axis=1)

    row = lax.broadcasted_iota(jnp.int32, (q, q), 0)
    col = lax.broadcasted_iota(jnp.int32, (q, q), 1)
    masks = (row >= col, row <= col)
    tris = (masks[0].astype(F32), masks[1].astype(F32))
    lane = lax.broadcasted_iota(jnp.int32, (q, 128), 1)
    a_r = -jnp.exp(ar_ref[...])
    per = 128 // hdim
    pad = jnp.zeros((q - 9 * heads, q), F32)

    def chunk_local(ci, d):
        rows = slice(ci * q, (ci + 1) * q)
        hs = slice(d * heads, (d + 1) * heads)
        dt = _softplus(dtr_ref[ci] + br_ref[...])[hs, :]
        cum = jnp.dot(dt * a_r[hs, :], tris[1 - d], precision=lax.Precision.HIGHEST, preferred_element_type=F32)
        bmt = bmt_scr[ci]
        scores = jnp.dot(cm_scr[rows, :], bmt, preferred_element_type=F32)
        yield
        end = q - 1 if d == 0 else 0
        cum_j = cum - jnp.log(dt)
        v = jnp.concatenate([cum, jnp.exp(cum), dt * jnp.exp(cum[:, end:end + 1] - cum)], axis=0)
        hi = v.astype(BF16).astype(F32)
        mid = (v - hi).astype(BF16).astype(F32)
        lo = (v - hi - mid).astype(BF16).astype(F32)
        vt = jnp.concatenate([hi, mid, lo, pad], axis=0).T.astype(BF16)
        yield
        cum_i = jnp.dot(vt, sela_scr[...], preferred_element_type=F32)
        spread = jnp.dot(vt, selb_scr[...], preferred_element_type=F32)
        yield
        x = xs_scr[rows, :]
        ec_scr[d, ci] = spread[:, :rp]
        st_scr[d, ci] = jnp.dot(bmt, (x * spread[:, rp:]).astype(BF16), preferred_element_type=F32)
        y_blocks = []
        for g in range(heads // per):
            xg = x[:, g * 128:(g + 1) * 128]
            w_parts, x_parts = [], []
            for k in range(per):
                r = g * per + k
                seg_ij = cum_i[:, r * q:(r + 1) * q] - cum_j[r:r + 1, :]
                w = scores * jnp.exp(jnp.where(masks[d], seg_ij, -jnp.inf))
                w_parts.append(w.astype(BF16))
                own = jnp.logical_and(lane >= k * hdim, lane < (k + 1) * hdim)
                x_parts.append(jnp.where(own, xg, 0.0).astype(BF16))
            y_blocks.append(jnp.dot(jnp.concatenate(w_parts, axis=1), jnp.concatenate(x_parts, axis=0),
                                    preferred_element_type=F32))
            yield
        y_ref[rows, :] += jnp.concatenate(y_blocks, axis=1)

    for t in range(n_chunks // LOCKSTEP_CHUNKS):
        chains = [chunk_local(t * LOCKSTEP_CHUNKS + k, d) for k in range(LOCKSTEP_CHUNKS) for d in range(2)]
        for _ in itertools.zip_longest(*chains):
            pass

    def carry_state(ci, d, restart):
        rows = slice(ci * q, (ci + 1) * q)
        end = q - 1 if d == 0 else 0
        h_in = ht_scr[d]
        if restart:
            h_in = jnp.where(is_ctx, 0.0, h_in)
        e_cum = ec_scr[d, ci]
        y_ref[rows, :] += jnp.dot(cm_scr[rows, :], h_in.astype(BF16), preferred_element_type=F32) * e_cum
        ht_scr[d] = e_cum[end:end + 1, :] * h_in + st_scr[d, ci]

    for s in range(n_seqs):
        for k in range(seq_chunks):
            ci = s * seq_chunks + k
            carry_state(ci, 0, k == 0)
            carry_state(n_chunks - 1 - ci, 1, k == 0)
        st_ref[s, 0] = ht_scr[0].T
        st_ref[n_seqs - 1 - s, 1] = ht_scr[1].T


def _ssd(xd, conv_w, conv_b, dtr, dt_bias, a_log, d_skip, state_ssm, layer, *, d_ssm, groups, n_state,
         n_ctx_rows, seq, block_rows):
    t = xd.shape[0]
    two_r = dtr.shape[-2]
    heads = two_r // 2
    rp = d_ssm // groups
    hdim = rp // heads
    assert 128 % hdim == 0 and heads % (128 // hdim) == 0 and rp % 128 == 0 and 9 * heads <= CHUNK
    assert block_rows % (LOCKSTEP_CHUNKS * CHUNK) == 0
    n_ctx_blk = n_ctx_rows // block_rows
    seqs = block_rows // seq
    n_chunks = block_rows // CHUNK
    nb = d_ssm // n_state
    kw, d_xbc = conv_w.shape
    dec_b, depth = state_ssm.shape[:2]
    h0 = state_ssm.reshape(dec_b, depth, 2, d_ssm, n_state)
    bias = dt_bias.reshape(2, groups, heads).transpose(1, 0, 2).reshape(groups, two_r, 1)
    alog = a_log.reshape(2, groups, heads).transpose(1, 0, 2).reshape(groups, two_r, 1)
    dsk = jnp.repeat(d_skip, hdim).reshape(groups, 1, rp)
    cb = conv_b.reshape(1, d_xbc)
    kern = functools.partial(_ssd_kernel, n_ctx_blk=n_ctx_blk, heads=heads, hdim=hdim, seq=seq,
                             seq_chunks=seq // CHUNK)
    return pl.pallas_call(
        kern,
        grid=(t // block_rows, groups),
        in_specs=[pl.BlockSpec((block_rows, rp), lambda b, g: (b, g)),
                  pl.BlockSpec((block_rows, n_state), lambda b, g: (b, nb + g)),
                  pl.BlockSpec((block_rows, n_state), lambda b, g: (b, nb + groups + g)),
                  pl.BlockSpec((kw, rp), lambda b, g: (0, g)),
                  pl.BlockSpec((kw, n_state), lambda b, g: (0, nb + g)),
                  pl.BlockSpec((kw, n_state), lambda b, g: (0, nb + groups + g)),
                  pl.BlockSpec((1, rp), lambda b, g: (0, g)),
                  pl.BlockSpec((1, n_state), lambda b, g: (0, nb + g)),
                  pl.BlockSpec((1, n_state), lambda b, g: (0, nb + groups + g)),
                  pl.BlockSpec((None, n_chunks, two_r, CHUNK), lambda b, g: (g, b, 0, 0)),
                  pl.BlockSpec((None, two_r, 1), lambda b, g: (g, 0, 0)),
                  pl.BlockSpec((None, two_r, 1), lambda b, g: (g, 0, 0)),
                  pl.BlockSpec((None, 1, rp), lambda b, g: (g, 0, 0)),
                  pl.BlockSpec((None, None, 2, rp, n_state),
                               lambda b, g: (jnp.maximum(b - n_ctx_blk, 0), layer, 0, g, 0))],
        out_specs=[pl.BlockSpec((block_rows, rp), lambda b, g: (b, g)),
                   pl.BlockSpec((seqs, 2, rp, n_state), lambda b, g: (b, 0, g, 0))],
        out_shape=[jax.ShapeDtypeStruct((t, d_ssm), F32),
                   jax.ShapeDtypeStruct((t // seq, 2, d_ssm, n_state), F32)],
        scratch_shapes=[pltpu.VMEM((block_rows, rp), F32), pltpu.VMEM((block_rows, n_state), F32),
                        pltpu.VMEM((n_chunks, n_state, CHUNK), BF16), pltpu.VMEM((block_rows, n_state), BF16),
                        pltpu.VMEM((2, n_state, rp), F32),
                        pltpu.VMEM((CHUNK, heads * CHUNK), BF16), pltpu.VMEM((CHUNK, 2 * rp), BF16),
                        pltpu.VMEM((2, n_chunks, CHUNK, rp), F32), pltpu.VMEM((2, n_chunks, n_state, rp), F32)],
        compiler_params=_params("parallel", "parallel"),
    )(xd, xd, xd, conv_w, conv_w, conv_w, cb, cb, cb, dtr, bias, alog, dsk, h0)


def kernel(x_prompt, x_sample, state_ssm, c, c_ctx, w_mod, b_mod, g_pre_mix, g_post_mix, g_pre_ffn, g_post_ffn, w_in, ssm_conv_w, ssm_conv_b, dt_bias, a_log, d_skip, g_ssm_norm, w_br_ssm, cv_conv_w, cv_conv_b, cv_ln_g, cv_ln_b, w_cv_out, w_out, w_up, ffn_conv_w, ffn_conv_b, w_down):
    batch, seq, d = x_prompt.shape
    dec_b, dec_seq, _ = x_sample.shape
    depth = w_mod.shape[0]
    n_state = state_ssm.shape[-1]
    heads_all = dt_bias.shape[-1]
    d_ssm = g_ssm_norm.shape[-1]
    d_xbc = ssm_conv_w.shape[-1]
    d_conv = cv_conv_w.shape[-1]
    groups = (d_xbc - d_ssm) // (2 * n_state)
    heads = heads_all // groups
    n_ctx_rows = batch * seq
    t = n_ctx_rows + dec_b * dec_seq
    assert seq % ROW_BLOCK == 0 and dec_seq % seq == 0 and dec_seq % GRID_W == 0 and ROW_BLOCK % GRID_W == 0
    assert n_ctx_rows % dec_seq == 0 and 1 + dec_b <= MOD_ROWS
    assert seq & (seq - 1) == 0 and GRID_W & (GRID_W - 1) == 0 and GRID_W % CONV_PIECE == 0
    geo = (n_ctx_rows // ROW_BLOCK, dec_seq // ROW_BLOCK)
    z_end = d_ssm
    dt_end = z_end + d_xbc + 2 * heads_all
    in_cols = w_in.shape[-1]

    cvec = jnp.concatenate([c_ctx[None, :], c, jnp.zeros((MOD_ROWS - 1 - dec_b, d), F32)], axis=0)
    mods = _modulation(cvec, w_mod, b_mod).reshape(depth, MOD_ROWS * 6, 1, d)

    x, h = _prenorm(x_prompt.reshape(n_ctx_rows, d), x_sample.reshape(dec_b * dec_seq, d), g_pre_mix[0], mods[0],
                    1, 0, geo)
    states = []
    for l in range(depth):
        z = _matmul_wcast(h, w_in, l, 0, z_end, tm=512, tn=1024)
        xd = _matmul_wcast(h, w_in, l, z_end, d_xbc, tm=512, tn=1024)
        dtm = _matmul_wcast(h, w_in, l, z_end + d_xbc, 2 * heads_all, tm=1024, tn=128)
        gg = _matmul_wcast(h, w_in, l, dt_end, in_cols - dt_end, tm=512, tn=1024)

        dt_raw = dtm.reshape(t, 2, groups, heads)
        dtr = dt_raw.transpose(2, 1, 3, 0).reshape(groups, 2 * heads, t // CHUNK, CHUNK).transpose(0, 2, 1, 3)
        y, st = _ssd(xd, ssm_conv_w[l], ssm_conv_b[l], dtr, dt_bias[l], a_log[l], d_skip[l], state_ssm, l,
                     d_ssm=d_ssm, groups=groups, n_state=n_state, n_ctx_rows=n_ctx_rows, seq=seq,
                     block_rows=dec_seq)
        states.append(st)
        yn = _gated_norm(y, z, g_ssm_norm[l])

        u = _glu_conv(gg, cv_conv_w[l], cv_conv_b[l], n_ctx_rows, seq, tc=512)
        ucv = _ln_silu(u, cv_ln_g[l], cv_ln_b[l])

        merged = _merge(yn, ucv, w_br_ssm[l].astype(BF16), w_cv_out[l].astype(BF16), gg, 2 * d_conv,
                        tm=512, tn=512)
        m = _matmul_wcast(merged, w_out, l, 0, d, tm=512, tn=1024)
        x, h = _resnorm(x, m, g_post_mix[l], mods[l], 2, g_pre_ffn[l], mods[l], 4, 3, geo)

        up = _matmul_wcast(h, w_up, l, 0, w_up.shape[-1], tm=1024, tn=512)
        act = _ffn_act(up, ffn_conv_w[l], ffn_conv_b[l], n_ctx_rows, seq, tr=256, tc=5504)
        f = _matmul(act, w_down[l].astype(BF16), tm=512, tn=512)
        if l + 1 < depth:
            x, h = _resnorm(x, f, g_post_ffn[l], mods[l], 5, g_pre_mix[l + 1], mods[l + 1], 1, 0, geo)
        else:
            x = _res(x, f, g_post_ffn[l], mods[l], 5, geo)

    y_prompt = x[:n_ctx_rows].reshape(batch, seq, d)
    y_sample = x[n_ctx_rows:].reshape(dec_b, dec_seq, d)
    new_state = jnp.stack([s[:batch] for s in states], axis=1).reshape(
        batch, depth, 2, heads_all, d_ssm // heads_all, n_state)
    return (y_prompt, y_sample, new_state)
```
